```python
import math
import jax, jax.numpy as jnp
from jax import lax
import numpy as np

D_MODEL = 1024
BATCH = 32
SEQ = 256
DEPTH = 1
DEC_BATCH = 8
DEC_SEQ = 4096
PAST_LEN = 256

GRID_W = 64
N_HEADS_A = 4
HD_A = 64
VD_A = 2 * HD_A
W_A = N_HEADS_A * VD_A
QK_COLS = N_HEADS_A * 2 * HD_A
N_HEADS_R = 8
HD_R = 64
W_R = N_HEADS_R * HD_R
LORA_W = 64
LORA_A = 64
D_FF = int(math.ceil(8 * D_MODEL / 3 / 256)) * 256
ROPE_THETA = 10000.0
Q_BLOCK = 128
EPS_RMS = 1e-6
EPS_GN = 64e-5
SPLIT_SIZES = (QK_COLS, QK_COLS, W_A, W_R, W_R, W_R, W_R, 2 * LORA_W, 2 * LORA_A, 2 * D_MODEL)
SPLIT_POINTS = [int(s) for s in np.cumsum(SPLIT_SIZES)[:-1]]
IN_COLS = int(sum(SPLIT_SIZES))

kernel_name = "diffusion_diffattn_rwkv7_hybrid_step"


def rms_norm(x, w, eps=EPS_RMS):
    xf = x.astype(jnp.float32)
    y = xf * lax.rsqrt(jnp.mean(xf * xf, axis=-1, keepdims=True) + eps)
    return (y * w.astype(jnp.float32)).astype(x.dtype)


def modulation(cond, ada_w, ada_b):
    m = jax.nn.silu(cond) @ ada_w + ada_b
    return jnp.split(m[..., None, :], 6, axis=-1)


def grid_angles(T):
    rows = T // GRID_W
    row = jnp.broadcast_to(jnp.arange(rows)[:, None], (rows, GRID_W)).reshape(-1).astype(jnp.float32)
    col = jnp.broadcast_to(jnp.arange(GRID_W)[None, :], (rows, GRID_W)).reshape(-1).astype(jnp.float32)
    nf = HD_A // 4
    inv = ROPE_THETA ** (-jnp.arange(nf, dtype=jnp.float32) / nf)
    return row[:, None] * inv, col[:, None] * inv


def rope_axial(x, ang_row, ang_col):
    def rot(xh, ang):
        cos = jnp.cos(ang)[None, :, None, None, :].astype(xh.dtype)
        sin = jnp.sin(ang)[None, :, None, None, :].astype(xh.dtype)
        x1, x2 = jnp.split(xh, 2, axis=-1)
        return jnp.concatenate([x1 * cos - x2 * sin, x1 * sin + x2 * cos], axis=-1)
    xr, xc = jnp.split(x, 2, axis=-1)
    return jnp.concatenate([rot(xr, ang_row), rot(xc, ang_col)], axis=-1)


def diff_attention(q, k, v, lam):
    B, Tq = q.shape[0], q.shape[1]
    nb = Tq // Q_BLOCK
    qb = q.reshape(B, nb, Q_BLOCK, N_HEADS_A, 2, HD_A).transpose(1, 0, 2, 3, 4, 5)
    kf = k.astype(jnp.float32)
    vf = v.astype(jnp.float32)
    scale = HD_A ** -0.5

    def block(qblk):
        s = jnp.einsum('bqhmd,bkhmd->bhmqk', qblk.astype(jnp.float32), kf) * scale
        p = jax.nn.softmax(s, axis=-1)
        a = p[:, :, 0] - lam * p[:, :, 1]
        return jnp.einsum('bhqk,bkhe->bqhe', a, vf)

    o = lax.map(block, qb)
    return o.transpose(1, 0, 2, 3, 4).reshape(B, Tq, N_HEADS_A, VD_A)


def rwkv_scan(s0, r, w, kk, b, k, v):
    def to_time(t):
        t = jnp.stack([t[:, 0], jnp.flip(t[:, 1], axis=1)], axis=1)
        return jnp.moveaxis(t, 2, 0)

    def step(S, inp):
        r_t, w_t, kk_t, b_t, k_t, v_t = inp
        sa = jnp.einsum('bzhvk,bzhk->bzhv', S, kk_t)
        S = S * w_t[..., None, :] - sa[..., :, None] * b_t[..., None, :] + v_t[..., :, None] * k_t[..., None, :]
        return S, jnp.einsum('bzhvk,bzhk->bzhv', S, r_t)

    xs = (to_time(r), to_time(w), to_time(kk), to_time(b), to_time(k), to_time(v))
    s_fin, ys = lax.scan(step, s0, xs)
    ys = jnp.moveaxis(ys, 0, 2)
    ys = jnp.stack([ys[:, 0], jnp.flip(ys[:, 1], axis=1)], axis=1)
    return ys, s_fin


def rwkv_mixer(r, k, v, g, wl, al, s0, lp):
    B, T = r.shape[0], r.shape[1]
    f32 = jnp.float32
    dt = r.dtype
    r = r.astype(f32)
    k = k.astype(f32)
    v = v.astype(f32)
    wl = jnp.tanh(wl.astype(f32)).reshape(B, T, 2, LORA_W)
    al = al.astype(f32).reshape(B, T, 2, LORA_A)
    w_log = -jax.nn.softplus(-(lp['w0'][None, :, None, :] + jnp.einsum('btzl,zlc->bztc', wl, lp['w_lora_up']))) - 0.5
    decay = jnp.exp(-jnp.exp(w_log))
    a = jax.nn.sigmoid(lp['a0'][None, :, None, :] + jnp.einsum('btzl,zlc->bztc', al, lp['a_lora_up']))
    kk = (k * lp['k_k']).reshape(B, T, N_HEADS_R, HD_R)
    kk = kk / jnp.maximum(jnp.sqrt(jnp.sum(kk * kk, axis=-1, keepdims=True)), 1e-12)
    kk = jnp.broadcast_to(kk.reshape(B, 1, T, W_R), (B, 2, T, W_R))
    k_d = k[:, None] * (1.0 + (a - 1.0) * lp['k_a'])
    r_b = jnp.broadcast_to(r[:, None], (B, 2, T, W_R))
    v_b = jnp.broadcast_to(v[:, None], (B, 2, T, W_R))

    def heads(t):
        return t.reshape(B, 2, T, N_HEADS_R, HD_R)

    ys, s_fin = rwkv_scan(s0.astype(f32), heads(r_b), heads(decay), heads(kk), heads(kk * a), heads(k_d), heads(v_b))
    y = ys.sum(axis=1)
    mu = jnp.mean(y, axis=-1, keepdims=True)
    var = jnp.mean(jnp.square(y - mu), axis=-1, keepdims=True)
    y = ((y - mu) * lax.rsqrt(var + EPS_GN)).reshape(B, T, W_R) * lp['ln_x_w'] + lp['ln_x_b']
    bonus = (jnp.sum(heads(r_b) * heads(k_d) * lp['r_k'], axis=-1, keepdims=True) * heads(v_b)).sum(axis=1)
    out = (y + bonus.reshape(B, T, W_R)) * jax.nn.sigmoid(g.astype(f32))
    return out.astype(dt), s_fin


def trunk_layer(x, cond, lp, li, ctx_k=None, ctx_v=None, ctx_state=None):
    latent = ctx_k is not None
    B, T, _ = x.shape
    sh1, sc1, g1, sh2, sc2, g2 = modulation(cond, lp['ada_w'], lp['ada_b'])
    h = rms_norm(x, lp['norm1_w']) * (1.0 + sc1) + sh1
    z = h @ lp['w_in']
    q, k, v, rr, kr, vr, gr, wl, al, bg = jnp.split(z, SPLIT_POINTS, axis=-1)

    q = rms_norm(q.reshape(B, T, N_HEADS_A, 2, HD_A), lp['q_norm_w'])
    k = rms_norm(k.reshape(B, T, N_HEADS_A, 2, HD_A), lp['k_norm_w'])
    v = v.reshape(B, T, N_HEADS_A, VD_A)
    own_k, own_v = k, v
    if latent:
        ang_r, ang_c = grid_angles(T)
        q = rope_axial(q, ang_r, ang_c)
        k = jnp.concatenate([ctx_k, rope_axial(k, ang_r, ang_c)], axis=1)
        v = jnp.concatenate([ctx_v, v], axis=1)
        s0 = ctx_state
    else:
        s0 = jnp.zeros((B, 2, N_HEADS_R, HD_R, HD_R), jnp.float32)
    lam_init = 0.8 - 0.6 * math.exp(-0.3 * li)
    f32 = jnp.float32
    lam = (jnp.exp(jnp.sum(lp['lambda_q1'].astype(f32) * lp['lambda_k1'].astype(f32)))
           - jnp.exp(jnp.sum(lp['lambda_q2'].astype(f32) * lp['lambda_k2'].astype(f32))) + lam_init)
    o_a = diff_attention(q, k, v, lam)
    o_a = (rms_norm(o_a, lp['subln_w']) * (1.0 - lam_init)).reshape(B, T, W_A).astype(x.dtype)

    o_r, s_fin = rwkv_mixer(rr, kr, vr, gr, wl, al, s0, lp)

    gate_a, gate_r = jnp.split(jax.nn.sigmoid(bg), 2, axis=-1)
    merged = gate_a * (o_a @ lp['w_attn_br']) + gate_r * (o_r @ lp['w_rwkv_br'])
    x = x + g1 * (merged @ lp['w_out'])

    h2 = rms_norm(x, lp['norm2_w']) * (1.0 + sc2) + sh2
    u, gt = jnp.split(h2 @ lp['w_ffn_in'], 2, axis=-1)
    x = x + g2 * ((jax.nn.silu(u) * gt) @ lp['w_ffn_out'])
    return x, own_k, own_v, s_fin


def setup_inputs(seed: int = 0) -> dict:
    key = jax.random.key(seed)
    ks = iter(jax.random.split(key, 48))

    def nrm(shape, scale):
        return jax.random.normal(next(ks), shape, jnp.float32) * scale

    L = DEPTH
    return {
        'x_prompt': nrm((BATCH, SEQ, D_MODEL), 1.0),
        'x_sample': nrm((DEC_BATCH, DEC_SEQ, D_MODEL), 1.0),
        'cache_k': nrm((DEC_BATCH, L, PAST_LEN, N_HEADS_A, 2, HD_A), 1.0),
        'cache_v': nrm((DEC_BATCH, L, PAST_LEN, N_HEADS_A, VD_A), 1.0),
        'state_rwkv': nrm((DEC_BATCH, L, 2, N_HEADS_R, HD_R, HD_R), 1.0),
        'c': nrm((DEC_BATCH, D_MODEL), 1.0),
        'c_ctx': nrm((D_MODEL,), 1.0),
        'ada_w': nrm((L, D_MODEL, 6 * D_MODEL), 0.5 * D_MODEL ** -0.5),
        'ada_b': nrm((L, 6 * D_MODEL), 0.02),
        'norm1_w': 1.0 + nrm((L, D_MODEL), 0.02),
        'norm2_w': 1.0 + nrm((L, D_MODEL), 0.02),
        'w_in': nrm((L, D_MODEL, IN_COLS), D_MODEL ** -0.5),
        'q_norm_w': 1.0 + nrm((L, HD_A), 0.02),
        'k_norm_w': 1.0 + nrm((L, HD_A), 0.02),
        'lambda_q1': nrm((L, HD_A), 0.1),
        'lambda_k1': nrm((L, HD_A), 0.1),
        'lambda_q2': nrm((L, HD_A), 0.1),
        'lambda_k2': nrm((L, HD_A), 0.1),
        'subln_w': 1.0 + nrm((L, VD_A), 0.02),
        'w_lora_up': nrm((L, 2, LORA_W, W_R), 0.1),
        'w0': jax.random.uniform(next(ks), (L, 2, W_R), jnp.float32, minval=-5.0, maxval=0.5),
        'a_lora_up': nrm((L, 2, LORA_A, W_R), 0.1),
        'a0': nrm((L, 2, W_R), 0.1),
        'k_k': 0.85 + nrm((L, W_R), 0.02),
        'k_a': 1.0 + nrm((L, W_R), 0.02),
        'r_k': nrm((L, N_HEADS_R, HD_R), 0.1),
        'ln_x_w': 1.0 + nrm((L, W_R), 0.02),
        'ln_x_b': nrm((L, W_R), 0.02),
        'w_attn_br': nrm((L, W_A, D_MODEL), W_A ** -0.5),
        'w_rwkv_br': nrm((L, W_R, D_MODEL), W_R ** -0.5),
        'w_out': nrm((L, D_MODEL, D_MODEL), D_MODEL ** -0.5),
        'w_ffn_in': nrm((L, D_MODEL, 2 * D_FF), D_MODEL ** -0.5),
        'w_ffn_out': nrm((L, D_FF, D_MODEL), D_FF ** -0.5),
    }


def reference(x_prompt, x_sample, cache_k, cache_v, state_rwkv, c, c_ctx, ada_w, ada_b, norm1_w, norm2_w,
              w_in, q_norm_w, k_norm_w, lambda_q1, lambda_k1, lambda_q2, lambda_k2, subln_w, w_lora_up, w0,
              a_lora_up, a0, k_k, k_a, r_k, ln_x_w, ln_x_b, w_attn_br, w_rwkv_br, w_out, w_ffn_in, w_ffn_out):
    y_prompt = x_prompt
    y_sample = x_sample
    ks_out, vs_out, ss_out = [], [], []
    for li in range(DEPTH):
        lp = {
            'ada_w': ada_w[li], 'ada_b': ada_b[li], 'norm1_w': norm1_w[li], 'norm2_w': norm2_w[li],
            'w_in': w_in[li], 'q_norm_w': q_norm_w[li], 'k_norm_w': k_norm_w[li],
            'lambda_q1': lambda_q1[li], 'lambda_k1': lambda_k1[li], 'lambda_q2': lambda_q2[li],
            'lambda_k2': lambda_k2[li], 'subln_w': subln_w[li], 'w_lora_up': w_lora_up[li], 'w0': w0[li],
            'a_lora_up': a_lora_up[li], 'a0': a0[li], 'k_k': k_k[li], 'k_a': k_a[li], 'r_k': r_k[li],
            'ln_x_w': ln_x_w[li], 'ln_x_b': ln_x_b[li], 'w_attn_br': w_attn_br[li], 'w_rwkv_br': w_rwkv_br[li],
            'w_out': w_out[li], 'w_ffn_in': w_ffn_in[li], 'w_ffn_out': w_ffn_out[li],
        }
        y_prompt, k_c, v_c, s_c = trunk_layer(y_prompt, c_ctx, lp, li)
        ks_out.append(k_c)
        vs_out.append(v_c)
        ss_out.append(s_c)
        y_sample, _, _, _ = trunk_layer(y_sample, c, lp, li, cache_k[:, li], cache_v[:, li], state_rwkv[:, li])
    new_k = jnp.stack(ks_out, axis=1)
    new_v = jnp.stack(vs_out, axis=1)
    new_state = jnp.stack(ss_out, axis=1)
    return (y_prompt, y_sample, new_k, new_v, new_state)
```

```python
import functools
import math

import jax
import jax.numpy as jnp
from jax import lax
from jax.experimental import pallas as pl
from jax.experimental.pallas import tpu as pltpu

F32 = jnp.float32
BF16 = jnp.bfloat16

N_HEADS_A = 4
HD_A = 64
VD_A = 2 * HD_A
N_HEADS_R = 8
HD_R = 64
W_R = N_HEADS_R * HD_R
LORA = 64
GRID_W = 64
ROPE_THETA = 10000.0
EPS_RMS = 1e-6
EPS_GN = 64e-5

LANES = 128
VMEM_LIMIT = 48 * 1024 * 1024

CHUNK = 64
SUB = 8
COL = 512

NN = (((1,), (0,)), ((), ()))
NT = (((1,), (1,)), ((), ()))
TN = (((0,), (0,)), ((), ()))


def _dot(a, b, dims=NN):
    return lax.dot_general(a.astype(BF16), b.astype(BF16), dims, preferred_element_type=F32)


def _split2(x):
    hi = x.astype(BF16)
    lo = (x - hi.astype(F32)).astype(BF16)
    return hi, lo


def _split3(x):
    hi = x.astype(BF16)
    r1 = x - hi.astype(F32)
    mid = r1.astype(BF16)
    lo = (r1 - mid.astype(F32)).astype(BF16)
    return hi, mid, lo


def _dot_x3(a, b, dims=NN):
    ah, al = _split2(a)
    bh, bl = _split2(b)
    d = functools.partial(lax.dot_general, dimension_numbers=dims, preferred_element_type=F32)
    return d(ah, bh) + d(ah, bl) + d(al, bh)


def _sigmoid(x):
    return 1.0 / (1.0 + jnp.exp(-x))


def _idiv(x, pow2):
    assert pow2 & (pow2 - 1) == 0
    return x >> int(math.log2(pow2))


def _group_ones(n, group):
    r = _idiv(lax.broadcasted_iota(jnp.int32, (n, n), 0), group)
    c = _idiv(lax.broadcasted_iota(jnp.int32, (n, n), 1), group)
    return jnp.where(r == c, 1.0, 0.0).astype(BF16)


def _group_sum(x, ones_bd):
    outs = []
    for cb in range(x.shape[1] // LANES):
        hi, lo = _split2(x[:, cb * LANES:(cb + 1) * LANES])
        outs.append(jnp.dot(hi, ones_bd, preferred_element_type=F32) + jnp.dot(lo, ones_bd, preferred_element_type=F32))
    return outs[0] if len(outs) == 1 else jnp.concatenate(outs, axis=1)


def _cparams(sem):
    return pltpu.CompilerParams(dimension_semantics=sem, vmem_limit_bytes=VMEM_LIMIT)


def _mod_kernel(c_ref, w_ref, b_ref, o_ref):
    c = c_ref[...]
    s = c * _sigmoid(c)
    o_ref[...] = _dot_x3(s, w_ref[...]) + b_ref[...]


def _modulation(cond, ada_w, ada_b):
    rows, d = cond.shape
    n = ada_w.shape[1]
    tn = 1536
    return pl.pallas_call(
        _mod_kernel,
        grid=(n // tn,),
        in_specs=[pl.BlockSpec((rows, d), lambda j: (0, 0)),
                  pl.BlockSpec((d, tn), lambda j: (0, j)),
                  pl.BlockSpec((1, tn), lambda j: (0, j))],
        out_specs=pl.BlockSpec((rows, tn), lambda j: (0, j)),
        out_shape=jax.ShapeDtypeStruct((rows, n), F32),
        compiler_params=_cparams(("parallel",)),
        name="modulation",
    )(cond, ada_w, ada_b.reshape(1, n))


def _rope_kernel(cos_ref, sin_ref, *, tm):
    i = pl.program_id(0)
    shift = int(math.log2(GRID_W))
    t = i * tm + lax.broadcasted_iota(jnp.int32, (tm, LANES), 0)
    lane = lax.broadcasted_iota(jnp.int32, (tm, LANES), 1)
    l64 = lane & (HD_A - 1)
    nf = HD_A // 4
    f = (l64 & (nf - 1)).astype(F32)
    inv = jnp.exp(f * (-math.log(ROPE_THETA) / nf))
    pos = jnp.where(l64 < HD_A // 2, t >> shift, t & (GRID_W - 1)).astype(F32)
    ang = pos * inv
    sn = jnp.sin(ang)
    cos_ref[...] = jnp.cos(ang)
    sin_ref[...] = jnp.where((l64 & (2 * nf - 1)) < nf, -sn, sn)


def _rope_tables(t_len):
    assert GRID_W & (GRID_W - 1) == 0
    tm = 512
    return pl.pallas_call(
        functools.partial(_rope_kernel, tm=tm),
        grid=(t_len // tm,),
        out_specs=[pl.BlockSpec((tm, LANES), lambda i: (i, 0))] * 2,
        out_shape=[jax.ShapeDtypeStruct((t_len, LANES), F32)] * 2,
        compiler_params=_cparams(("parallel",)),
        name="rope_tables",
    )()


TILE_Q, TILE_K, TILE_V, TILE_RR, TILE_KR, TILE_VR, TILE_GR, TILE_LORA, TILE_GATE = 0, 1, 2, 3, 4, 5, 6, 7, 8
N_TILES = 12


def _rope_apply(y, cos, sin):
    n = y.shape[1]
    lane = lax.broadcasted_iota(jnp.int32, y.shape, 1)
    quarter = HD_A // 4
    first = (lane & (2 * quarter - 1)) < quarter
    swapped = jnp.where(first, pltpu.roll(y, n - quarter, 1), pltpu.roll(y, quarter, 1))
    reps = n // LANES
    cos_f = jnp.concatenate([cos] * reps, axis=1)
    sin_f = jnp.concatenate([sin] * reps, axis=1)
    return y * cos_f + swapped * sin_f


def _inproj_kernel(*refs, latent):
    if latent:
        x_ref, mod_ref, nw_ref, w_ref, qkw_ref, cos_ref, sin_ref, z_ref, krot_ref, h_scr = refs
    else:
        x_ref, mod_ref, nw_ref, w_ref, qkw_ref, z_ref, h_scr = refs
    j = pl.program_id(2)
    d = x_ref.shape[2]

    @pl.when(j == 0)
    def _():
        x = x_ref[0]
        ms = jnp.mean(x * x, axis=-1, keepdims=True)
        y = x * lax.rsqrt(ms + EPS_RMS) * nw_ref[...]
        sh1 = mod_ref[0, :, 0:d]
        sc1 = mod_ref[0, :, d:2 * d]
        h_scr[...] = (y * (1.0 + sc1) + sh1).astype(BF16)

    z = jnp.dot(h_scr[...], w_ref[...], preferred_element_type=F32)

    def qk_norm(zz, w_row):
        ms = _group_sum(zz * zz, _group_ones(LANES, HD_A)) * (1.0 / HD_A)
        return zz * lax.rsqrt(ms + EPS_RMS) * w_row

    @pl.when(j == TILE_Q)
    def _():
        y = qk_norm(z, qkw_ref[0:1, :])
        if latent:
            y = _rope_apply(y, cos_ref[...], sin_ref[...])
        z_ref[0] = y

    @pl.when(j == TILE_K)
    def _():
        y = qk_norm(z, qkw_ref[1:2, :])
        z_ref[0] = y
        if latent:
            krot_ref[0] = _rope_apply(y, cos_ref[...], sin_ref[...])

    @pl.when(j >= TILE_GATE)
    def _():
        z_ref[0] = _sigmoid(z)

    @pl.when(jnp.logical_and(j > TILE_K, j < TILE_GATE))
    def _():
        z_ref[0] = z


def _inproj(x, mod, norm1_w, w_bf, qkw, rope, *, tm):
    nb, t_len, d = x.shape
    latent = rope is not None
    ncols = w_bf.shape[1]
    in_specs = [pl.BlockSpec((1, tm, d), lambda b, i, j: (b, i, 0)),
                pl.BlockSpec((1, 1, mod.shape[2]), lambda b, i, j: (b, 0, 0)),
                pl.BlockSpec((1, d), lambda b, i, j: (0, 0)),
                pl.BlockSpec((d, COL), lambda b, i, j: (0, j)),
                pl.BlockSpec((2, COL), lambda b, i, j: (0, 0))]
    args = [x, mod, norm1_w.reshape(1, d), w_bf, qkw]
    out_specs = [pl.BlockSpec((1, tm, COL), lambda b, i, j: (b, i, j))]
    out_shape = [jax.ShapeDtypeStruct((nb, t_len, ncols), F32)]
    if latent:
        in_specs += [pl.BlockSpec((tm, LANES), lambda b, i, j: (i, 0))] * 2
        args += list(rope)
        out_specs.append(pl.BlockSpec((1, tm, COL), lambda b, i, j: (b, i, 0)))
        out_shape.append(jax.ShapeDtypeStruct((nb, t_len, COL), F32))
    return pl.pallas_call(
        functools.partial(_inproj_kernel, latent=latent),
        grid=(nb, t_len // tm, ncols // COL),
        in_specs=in_specs,
        out_specs=out_specs,
        out_shape=out_shape,
        scratch_shapes=[pltpu.VMEM((tm, d), BF16)],
        compiler_params=_cparams(("parallel", "parallel", "arbitrary")),
        name="inproj_latent" if latent else "inproj_context",
    )(*args)


def _attn_kernel(*refs, has_cache, lam_init):
    if has_cache:
        q_ref, k_ref, v_ref, ck_ref, cv_ref, lam_ref, sw_ref, o_ref = refs
    else:
        q_ref, k_ref, v_ref, lam_ref, sw_ref, o_ref = refs
    scale = HD_A ** -0.5
    q = q_ref[0] * scale
    lane = lax.broadcasted_iota(jnp.int32, q.shape, 1)
    k = k_ref[0].astype(BF16)
    v = v_ref[0].astype(BF16)
    if has_cache:
        ck = ck_ref[0].astype(BF16)
        cv = cv_ref[0].astype(BF16)
    outs = []
    for m in range(2):
        qm = jnp.where(_idiv(lane, HD_A) == m, q, 0.0).astype(BF16)
        s = lax.dot_general(qm, k, NT, preferred_element_type=F32)
        mx = jnp.max(s, axis=-1, keepdims=True)
        if has_cache:
            sc = lax.dot_general(qm, ck, NT, preferred_element_type=F32)
            mx = jnp.maximum(mx, jnp.max(sc, axis=-1, keepdims=True))
        p = jnp.exp(s - mx)
        den = jnp.sum(p, axis=-1, keepdims=True)
        o = jnp.dot(p.astype(BF16), v, preferred_element_type=F32)
        if has_cache:
            pc = jnp.exp(sc - mx)
            den = den + jnp.sum(pc, axis=-1, keepdims=True)
            o = o + jnp.dot(pc.astype(BF16), cv, preferred_element_type=F32)
        outs.append(o / den)
    lp = lam_ref[...]
    lam = (jnp.exp(jnp.sum(lp[0:1] * lp[1:2], axis=-1, keepdims=True))
           - jnp.exp(jnp.sum(lp[2:3] * lp[3:4], axis=-1, keepdims=True)) + lam_init)
    o = outs[0] - lam * outs[1]
    ms = jnp.mean(o * o, axis=-1, keepdims=True)
    o_ref[0] = o * lax.rsqrt(ms + EPS_RMS) * sw_ref[...] * (1.0 - lam_init)


def _attention(z, kk_arr, k_col0, cache, lam_p, subln_w, lam_init, *, tq):
    nb, t_len, _ = z.shape
    cpb = COL // LANES
    in_specs = [pl.BlockSpec((1, tq, LANES), lambda b, h, i: (b, i, TILE_Q * cpb + h)),
                pl.BlockSpec((1, t_len, LANES), lambda b, h, i: (b, 0, k_col0 + h)),
                pl.BlockSpec((1, t_len, LANES), lambda b, h, i: (b, 0, TILE_V * cpb + h))]
    args = [z, kk_arr, z]
    if cache is not None:
        ck, cv = cache
        past = ck.shape[1]
        in_specs += [pl.BlockSpec((1, past, LANES), lambda b, h, i: (b, 0, h))] * 2
        args += [ck, cv]
    in_specs += [pl.BlockSpec((4, HD_A), lambda b, h, i: (0, 0)),
                 pl.BlockSpec((1, VD_A), lambda b, h, i: (0, 0))]
    args += [lam_p, subln_w.reshape(1, VD_A)]
    return pl.pallas_call(
        functools.partial(_attn_kernel, has_cache=cache is not None, lam_init=lam_init),
        grid=(nb, N_HEADS_A, t_len // tq),
        in_specs=in_specs,
        out_specs=pl.BlockSpec((1, tq, LANES), lambda b, h, i: (b, i, h)),
        out_shape=jax.ShapeDtypeStruct((nb, t_len, N_HEADS_A * VD_A), F32),
        compiler_params=_cparams(("parallel", "parallel", "arbitrary")),
        name="attention_latent" if cache is not None else "attention_context",
    )(*args)


def _prep_kernel(r_ref, k_ref, la_ref, wup_ref, aup_ref, w0_ref, a0_ref, kk_ref_p, ka_ref, rk_ref,
                 kk_out, lw_out, b_out, kd_out, bs_out):
    r = r_ref[0]
    k = k_ref[0]
    la = la_ref[0]
    ones_bd = _group_ones(LANES, HD_R)
    kraw = k * kk_ref_p[...]
    nrm = jnp.sqrt(_group_sum(kraw * kraw, ones_bd))
    kk = kraw / jnp.maximum(nrm, 1e-12)
    kk_out[0] = kk
    wl = jnp.tanh(la[:, 0:2 * LORA])
    al = la[:, 2 * LORA:4 * LORA]
    lane = lax.broadcasted_iota(jnp.int32, wl.shape, 1)
    bonus = jnp.zeros_like(r)
    for z in range(2):
        sel = _idiv(lane, LORA) == z
        w = w0_ref[z:z + 1, :] + _dot_x3(jnp.where(sel, wl, 0.0), wup_ref[...])
        nw = -w
        softplus = jnp.maximum(nw, 0.0) + jnp.log(1.0 + jnp.exp(-jnp.abs(nw)))
        w_log = -softplus - 0.5
        lw_out[z, 0] = -jnp.exp(w_log)
        a = _sigmoid(a0_ref[z:z + 1, :] + _dot_x3(jnp.where(sel, al, 0.0), aup_ref[...]))
        kd = k * (1.0 + (a - 1.0) * ka_ref[...])
        b_out[z, 0] = kk * a
        kd_out[z, 0] = kd
        bonus = bonus + _group_sum(r * kd * rk_ref[...], ones_bd)
    bs_out[0] = bonus


def _rwkv_prep(z, w_lora_up, a_lora_up, w0, a0, k_k, k_a, r_k, *, tm):
    nb, t_len, _ = z.shape
    tok = lambda col: pl.BlockSpec((1, tm, COL), lambda b, i: (b, i, col))
    par = lambda rows: pl.BlockSpec((rows, W_R), lambda b, i: (0, 0))
    dir_spec = pl.BlockSpec((2, 1, tm, W_R), lambda b, i: (0, b, i, 0))
    tok_out = pl.BlockSpec((1, tm, W_R), lambda b, i: (b, i, 0))
    one = jax.ShapeDtypeStruct((nb, t_len, W_R), F32)
    two = jax.ShapeDtypeStruct((2, nb, t_len, W_R), F32)
    return pl.pallas_call(
        _prep_kernel,
        grid=(nb, t_len // tm),
        in_specs=[tok(TILE_RR), tok(TILE_KR),
                  pl.BlockSpec((1, tm, 4 * LORA), lambda b, i: (b, i, TILE_LORA * COL // (4 * LORA))),
                  par(2 * LORA), par(2 * LORA), par(2), par(2), par(1), par(1), par(1)],
        out_specs=[tok_out, dir_spec, dir_spec, dir_spec, tok_out],
        out_shape=[one, two, two, two, one],
        compiler_params=_cparams(("parallel", "parallel")),
        name="rwkv_prep",
    )(z, z, z, w_lora_up.reshape(2 * LORA, W_R), a_lora_up.reshape(2 * LORA, W_R), w0, a0,
      k_k.reshape(1, W_R), k_a.reshape(1, W_R), r_k.reshape(1, W_R))


def _pair_rows(y):
    lane = lax.broadcasted_iota(jnp.int32, y.shape, 1) & (LANES - 1)
    return jnp.concatenate([jnp.where(lane < HD_R, y, 0.0).astype(BF16),
                            jnp.where(lane >= HD_R, y, 0.0).astype(BF16)], axis=0)


def _rwkv_chunk_kernel(rf, kkf, vf, lwf, bf_, kdf, rb, kkb, vb, lwb, bb_, kdb, s0_ref,
                       yf_ref, yb_ref, sfin_ref, s_scr, *, nbb):
    c = pl.program_id(1)
    n_pairs = W_R // LANES

    @pl.when(c == 0)
    def _():
        s_scr[...] = s0_ref[...]

    C = CHUNK
    t_i = lax.broadcasted_iota(jnp.int32, (C, C), 0)
    i_i = lax.broadcasted_iota(jnp.int32, (C, C), 1)
    t2 = lax.broadcasted_iota(jnp.int32, (C, LANES), 0)
    i2 = lax.broadcasted_iota(jnp.int32, (C, LANES), 1) & (HD_R - 1)
    same_sub = _idiv(t2, SUB) == _idiv(i2, SUB)
    eye = lax.broadcasted_iota(jnp.int32, (LANES, LANES), 0) == lax.broadcasted_iota(jnp.int32, (LANES, LANES), 1)
    same_head = (_idiv(lax.broadcasted_iota(jnp.int32, (LANES, LANES), 0), HD_R)
                 == _idiv(lax.broadcasted_iota(jnp.int32, (LANES, LANES), 1), HD_R))
    masks = {}
    for rev in (False, True):
        tri = (i_i >= t_i) if rev else (i_i <= t_i)
        masks[rev] = dict(tri=jnp.where(tri, 1.0, 0.0).astype(BF16),
                          strict=(i2 > t2) if rev else (i2 < t2),
                          incl=(i2 >= t2) if rev else (i2 <= t2))

    chains = []
    for bi in range(nbb):
        for rev, (r_, kk_, v_, lw_, b_, kd_) in ((False, (rf, kkf, vf, lwf, bf_, kdf)),
                                                  (True, (rb, kkb, vb, lwb, bb_, kdb))):
            for p in range(n_pairs):
                sl = slice(p * LANES, (p + 1) * LANES)
                chains.append(dict(bi=bi, rev=rev, p=p, r=r_[bi, :, sl], kk=kk_[bi, :, sl], v=v_[bi, :, sl],
                                   lw=lw_[bi, :, sl], b=b_[bi, :, sl], kd=kd_[bi, :, sl]))

    for ch in chains:
        h, m, l = _split3(ch['lw'])
        cl = jnp.dot(masks[ch['rev']]['tri'], jnp.concatenate([h, m, l], axis=1), preferred_element_type=F32)
        ch['cl'] = cl[:, 0:LANES] + cl[:, LANES:2 * LANES] + cl[:, 2 * LANES:3 * LANES]
    for ch in chains:
        cl, lw = ch['cl'], ch['lw']
        p_in = jnp.exp(cl)
        p_inv = jnp.exp(-cl)
        p_ex = jnp.exp(cl - lw)
        ch['pc'] = jnp.exp(jnp.sum(lw, axis=0, keepdims=True))
        p_end = ch['pc'] * p_inv
        ch['at'] = -ch['kk'] * p_ex
        ch['rt'] = ch['r'] * p_in
        ch['bt'] = ch['b'] * p_inv
        ch['kt'] = ch['kd'] * p_inv
        ch['bh'] = ch['b'] * p_end
        ch['kh'] = ch['kd'] * p_end
        ch['S'] = s_scr[ch['bi'], int(ch['rev']), ch['p']]
    for ch in chains:
        g = lax.dot_general(jnp.concatenate([ch['at'], ch['rt']], axis=0).astype(BF16),
                            jnp.concatenate([_pair_rows(ch['bt']), _pair_rows(ch['kt'])], axis=0),
                            NT, preferred_element_type=F32)
        mk = masks[ch['rev']]
        ch['lab'] = jnp.where(mk['strict'], g[0:C, 0:LANES], 0.0)
        ch['lak'] = jnp.where(mk['strict'], g[0:C, LANES:2 * LANES], 0.0)
        ch['lrb'] = jnp.where(mk['incl'], g[C:2 * C, 0:LANES], 0.0)
        ch['lrk'] = jnp.where(mk['incl'], g[C:2 * C, LANES:2 * LANES], 0.0)
        ch['vbd'] = _pair_rows(ch['v'])
    for ch in chains:
        x0 = jnp.dot(jnp.concatenate([ch['at'], ch['lak']], axis=1).astype(BF16),
                     jnp.concatenate([ch['S'].astype(BF16), ch['vbd']], axis=0), preferred_element_type=F32)
        nd = jnp.where(same_sub, ch['lab'], 0.0)
        no = jnp.where(same_sub, 0.0, ch['lab'])
        ch['zc'] = jnp.concatenate([x0, no], axis=1)
        ch['nj'] = nd
    n1 = int(math.log2(SUB))
    for lev in range(n1):
        for ch in chains:
            if lev < n1 - 1:
                rr = jnp.dot(ch['nj'].astype(BF16), _pair_rows(jnp.concatenate([ch['zc'], ch['nj']], axis=1)),
                             preferred_element_type=F32)
                ch['zc'] = ch['zc'] + rr[:, 0:2 * LANES]
                ch['nj'] = rr[:, 2 * LANES:3 * LANES]
            else:
                ch['zc'] = ch['zc'] + jnp.dot(ch['nj'].astype(BF16), _pair_rows(ch['zc']),
                                              preferred_element_type=F32)
    for ch in chains:
        ch['x'] = ch['zc'][:, 0:LANES]
        ch['mj'] = ch['zc'][:, LANES:2 * LANES]
    n2 = int(math.log2(C // SUB))
    for lev in range(n2):
        for ch in chains:
            if lev < n2 - 1:
                rr = jnp.dot(ch['mj'].astype(BF16), _pair_rows(jnp.concatenate([ch['x'], ch['mj']], axis=1)),
                             preferred_element_type=F32)
                ch['x'] = ch['x'] + rr[:, 0:LANES]
                ch['mj'] = rr[:, LANES:2 * LANES]
            else:
                ch['x'] = ch['x'] + jnp.dot(ch['mj'].astype(BF16), _pair_rows(ch['x']), preferred_element_type=F32)
    for ch in chains:
        u = ch['x']
        ubd = _pair_rows(u)
        y = jnp.dot(jnp.concatenate([ch['rt'], ch['lrb'], ch['lrk']], axis=1).astype(BF16),
                    jnp.concatenate([ch['S'].astype(BF16), ubd, ch['vbd']], axis=0), preferred_element_type=F32)
        dg = jnp.where(eye, jnp.broadcast_to(ch['pc'], (LANES, LANES)), 0.0)
        s_new = lax.dot_general(jnp.concatenate([ch['bh'], ch['kh'], dg], axis=0).astype(BF16),
                                jnp.concatenate([u, ch['v'], ch['S']], axis=0).astype(BF16),
                                TN, preferred_element_type=F32)
        s_new = jnp.where(same_head, s_new, 0.0)
        sl = slice(ch['p'] * LANES, (ch['p'] + 1) * LANES)
        if ch['rev']:
            yb_ref[ch['bi'], :, sl] = y
        else:
            yf_ref[ch['bi'], :, sl] = y
        s_scr[ch['bi'], int(ch['rev']), ch['p']] = s_new

    @pl.when(c == pl.num_programs(1) - 1)
    def _():
        sfin_ref[...] = s_scr[...]


def _rwkv_scan(r_src, r_col, v_col, kk, lw, bb, kd, s0, *, nbb):
    nb, t_len, _ = kk.shape
    nc = t_len // CHUNK
    n_pairs = W_R // LANES

    def tok(col, rev):
        return pl.BlockSpec((nbb, CHUNK, W_R), (lambda g, c: (g, nc - 1 - c, col)) if rev else (lambda g, c: (g, c, col)))

    def dirs(z, rev):
        return pl.BlockSpec((None, nbb, CHUNK, W_R),
                            (lambda g, c: (z, g, nc - 1 - c, 0)) if rev else (lambda g, c: (z, g, c, 0)))

    st_spec = pl.BlockSpec((nbb, 2, n_pairs, LANES, LANES), lambda g, c: (g, 0, 0, 0, 0))
    in_specs, args = [], []
    for rev in (False, True):
        z = int(rev)
        in_specs += [tok(r_col, rev), tok(0, rev), tok(v_col, rev), dirs(z, rev), dirs(z, rev), dirs(z, rev)]
        args += [r_src, kk, r_src, lw, bb, kd]
    in_specs.append(st_spec)
    args.append(s0)
    yshape = jax.ShapeDtypeStruct((nb, t_len, W_R), F32)
    return pl.pallas_call(
        functools.partial(_rwkv_chunk_kernel, nbb=nbb),
        grid=(nb // nbb, nc),
        in_specs=in_specs,
        out_specs=[tok(0, False), tok(0, True), st_spec],
        out_shape=[yshape, yshape, jax.ShapeDtypeStruct(s0.shape, F32)],
        scratch_shapes=[pltpu.VMEM((nbb, 2, n_pairs, LANES, LANES), F32)],
        compiler_params=_cparams(("parallel", "arbitrary")),
        name="rwkv_scan",
    )(*args)


def _merge_kernel(x_ref, yf_ref, yb_ref, bs_ref, vr_ref, gr_ref, oa_ref, ga_ref, gg_ref, mod_ref,
                  lnw_ref, lnb_ref, wa_ref, wr_ref, wo_ref, n2_ref, x1_ref, h2_ref):
    d = x_ref.shape[2]
    ones_bd = _group_ones(LANES, HD_R)
    y = yf_ref[0] + yb_ref[0]
    mu = _group_sum(y, ones_bd) * (1.0 / HD_R)
    yc = y - mu
    var = _group_sum(yc * yc, ones_bd) * (1.0 / HD_R)
    yn = yc * lax.rsqrt(var + EPS_GN) * lnw_ref[...] + lnb_ref[...]
    o_r = (yn + bs_ref[0] * vr_ref[0]) * _sigmoid(gr_ref[0])
    merged = ga_ref[0] * _dot(oa_ref[0], wa_ref[...]) + gg_ref[0] * _dot(o_r, wr_ref[...])
    g1 = mod_ref[0, :, 2 * d:3 * d]
    x1 = x_ref[0] + g1 * _dot(merged, wo_ref[...])
    x1_ref[0] = x1
    sh2 = mod_ref[0, :, 3 * d:4 * d]
    sc2 = mod_ref[0, :, 4 * d:5 * d]
    ms = jnp.mean(x1 * x1, axis=-1, keepdims=True)
    h2_ref[0] = (x1 * lax.rsqrt(ms + EPS_RMS) * n2_ref[...] * (1.0 + sc2) + sh2).astype(BF16)


def _merge(x, z, yf, yb, bsum, oa, mod, ln_x_w, ln_x_b, wa_bf, wr_bf, wo_bf, norm2_w, *, tm):
    nb, t_len, d = x.shape
    tok = lambda w, col: pl.BlockSpec((1, tm, w), lambda b, i: (b, i, col))
    full = lambda a: pl.BlockSpec(a.shape, lambda b, i: (0,) * a.ndim)
    lnw, lnb, n2 = ln_x_w.reshape(1, W_R), ln_x_b.reshape(1, W_R), norm2_w.reshape(1, d)
    gate_blk = TILE_GATE * COL // d
    return pl.pallas_call(
        _merge_kernel,
        grid=(nb, t_len // tm),
        in_specs=[tok(d, 0), tok(W_R, 0), tok(W_R, 0), tok(W_R, 0), tok(COL, TILE_VR), tok(COL, TILE_GR),
                  tok(W_R, 0), tok(d, gate_blk), tok(d, gate_blk + 1),
                  pl.BlockSpec((1, 1, mod.shape[2]), lambda b, i: (b, 0, 0)),
                  full(lnw), full(lnb), full(wa_bf), full(wr_bf), full(wo_bf), full(n2)],
        out_specs=[tok(d, 0), tok(d, 0)],
        out_shape=[jax.ShapeDtypeStruct((nb, t_len, d), F32), jax.ShapeDtypeStruct((nb, t_len, d), BF16)],
        compiler_params=_cparams(("parallel", "parallel")),
        name="merge",
    )(x, yf, yb, bsum, z, z, oa, z, z, mod, lnw, lnb, wa_bf, wr_bf, wo_bf, n2)


def _ffn_kernel(h_ref, x1_ref, mod_ref, wu_ref, wg_ref, wo_ref, o_ref, acc_ref):
    f = pl.program_id(2)
    d = x1_ref.shape[2]

    @pl.when(f == 0)
    def _():
        acc_ref[...] = jnp.zeros_like(acc_ref)

    h = h_ref[0]
    u = jnp.dot(h, wu_ref[...], preferred_element_type=F32)
    g = jnp.dot(h, wg_ref[...], preferred_element_type=F32)
    a = (u * _sigmoid(u)) * g
    acc_ref[...] += jnp.dot(a.astype(BF16), wo_ref[...], preferred_element_type=F32)

    @pl.when(f == pl.num_programs(2) - 1)
    def _():
        g2 = mod_ref[0, :, 5 * d:6 * d]
        o_ref[0] = x1_ref[0] + g2 * acc_ref[...]


def _ffn(h2, x1, mod, w_in_bf, w_out_bf, *, tm, tf):
    nb, t_len, d = x1.shape
    d_ff = w_out_bf.shape[0]
    nf = d_ff // tf
    return pl.pallas_call(
        _ffn_kernel,
        grid=(nb, t_len // tm, nf),
        in_specs=[pl.BlockSpec((1, tm, d), lambda b, i, f: (b, i, 0)),
                  pl.BlockSpec((1, tm, d), lambda b, i, f: (b, i, 0)),
                  pl.BlockSpec((1, 1, mod.shape[2]), lambda b, i, f: (b, 0, 0)),
                  pl.BlockSpec((d, tf), lambda b, i, f: (0, f)),
                  pl.BlockSpec((d, tf), lambda b, i, f: (0, nf + f)),
                  pl.BlockSpec((tf, d), lambda b, i, f: (f, 0))],
        out_specs=pl.BlockSpec((1, tm, d), lambda b, i, f: (b, i, 0)),
        out_shape=jax.ShapeDtypeStruct((nb, t_len, d), F32),
        scratch_shapes=[pltpu.VMEM((tm, d), F32)],
        compiler_params=_cparams(("parallel", "parallel", "arbitrary")),
        name="ffn",
    )(h2, x1, mod, w_in_bf, w_in_bf, w_out_bf)


def _state_to_pairs(s):
    nb = s.shape[0]
    st = jnp.swapaxes(s, -1, -2).reshape(nb, 2, N_HEADS_R // 2, 2, HD_R, HD_R)
    zero = jnp.zeros_like(st[:, :, :, 0])
    top = jnp.concatenate([st[:, :, :, 0], zero], axis=-1)
    bot = jnp.concatenate([zero, st[:, :, :, 1]], axis=-1)
    return jnp.concatenate([top, bot], axis=-2)


def _pairs_to_state(sp):
    nb = sp.shape[0]
    h0 = sp[:, :, :, 0:HD_R, 0:HD_R]
    h1 = sp[:, :, :, HD_R:, HD_R:]
    st = jnp.stack([h0, h1], axis=3).reshape(nb, 2, N_HEADS_R, HD_R, HD_R)
    return jnp.swapaxes(st, -1, -2)


def _layer(x_tok, nb_seq, mod, lp, lam_init, cache, s0_pairs, rope, *, tm, tq, nbb):
    nbm, tmod, d = x_tok.shape
    t_seq = nbm * tmod // nb_seq
    outs = _inproj(x_tok, mod, lp['norm1_w'], lp['w_in'], lp['qkw'], rope, tm=tm)
    z = outs[0]
    zs = z.reshape(nb_seq, t_seq, z.shape[2])
    cpb = COL // LANES
    if cache is not None:
        krot = outs[1].reshape(nb_seq, t_seq, COL)
        oa = _attention(zs, krot, 0, cache, lp['lam'], lp['subln_w'], lam_init, tq=tq)
    else:
        oa = _attention(zs, zs, TILE_K * cpb, None, lp['lam'], lp['subln_w'], lam_init, tq=tq)
    kk, lw, bb, kd, bsum = _rwkv_prep(z, lp['w_lora_up'], lp['a_lora_up'], lp['w0'], lp['a0'],
                                      lp['k_k'], lp['k_a'], lp['r_k'], tm=512)
    seq = lambda a: a.reshape(a.shape[:-3] + (nb_seq, t_seq, a.shape[-1]))
    yf, yb, s_fin = _rwkv_scan(zs, TILE_RR, TILE_VR, seq(kk), seq(lw), seq(bb), seq(kd), s0_pairs, nbb=nbb)
    tokv = lambda a: a.reshape(nbm, tmod, a.shape[-1])
    x1, h2 = _merge(x_tok, z, tokv(yf), tokv(yb), bsum, tokv(oa), mod, lp['ln_x_w'], lp['ln_x_b'],
                    lp['w_attn_br'], lp['w_rwkv_br'], lp['w_out'], lp['norm2_w'], tm=512)
    y = _ffn(h2, x1, mod, lp['w_ffn_in'], lp['w_ffn_out'], tm=512, tf=lp['w_ffn_out'].shape[0] // 2)
    return y, zs, s_fin


def kernel(x_prompt, x_sample, cache_k, cache_v, state_rwkv, c, c_ctx, ada_w, ada_b, norm1_w, norm2_w, w_in, q_norm_w, k_norm_w, lambda_q1, lambda_k1, lambda_q2, lambda_k2, subln_w, w_lora_up, w0, a_lora_up, a0, k_k, k_a, r_k, ln_x_w, ln_x_b, w_attn_br, w_rwkv_br, w_out, w_ffn_in, w_ffn_out):
    depth = ada_w.shape[0]
    batch, seq_len, d = x_prompt.shape
    dec_batch, dec_seq, _ = x_sample.shape
    past = cache_k.shape[2]
    qk_cols = N_HEADS_A * 2 * HD_A
    assert qk_cols == COL and W_R == COL and N_HEADS_A * VD_A == COL

    rope = _rope_tables(dec_seq)
    y_prompt = x_prompt.reshape(1, batch * seq_len, d)
    y_sample = x_sample
    ks_out, vs_out, ss_out = [], [], []
    cond_rows = 16
    for li in range(depth):
        cond = jnp.concatenate([c_ctx[None, :], c, jnp.zeros((cond_rows - 1 - dec_batch, d), F32)], axis=0)
        m = _modulation(cond, ada_w[li], ada_b[li])
        mod_ctx = m[0:1].reshape(1, 1, 6 * d)
        mod_lat = m[1:1 + dec_batch].reshape(dec_batch, 1, 6 * d)

        wi = w_in[li]
        o_lora = 3 * COL + 4 * W_R
        o_gate = o_lora + 4 * LORA
        w_re = jnp.concatenate([wi[:, :o_lora], wi[:, o_lora:o_gate], jnp.zeros((d, COL - 4 * LORA), F32),
                                wi[:, o_gate:]], axis=1).astype(BF16)
        assert w_re.shape[1] == N_TILES * COL
        lp = {
            'norm1_w': norm1_w[li], 'norm2_w': norm2_w[li], 'w_in': w_re,
            'qkw': jnp.stack([jnp.tile(q_norm_w[li], 2 * N_HEADS_A), jnp.tile(k_norm_w[li], 2 * N_HEADS_A)]),
            'lam': jnp.stack([lambda_q1[li], lambda_k1[li], lambda_q2[li], lambda_k2[li]]),
            'subln_w': subln_w[li], 'w_lora_up': w_lora_up[li], 'w0': w0[li], 'a_lora_up': a_lora_up[li],
            'a0': a0[li], 'k_k': k_k[li], 'k_a': k_a[li], 'r_k': r_k[li], 'ln_x_w': ln_x_w[li],
            'ln_x_b': ln_x_b[li], 'w_attn_br': w_attn_br[li].astype(BF16), 'w_rwkv_br': w_rwkv_br[li].astype(BF16),
            'w_out': w_out[li].astype(BF16), 'w_ffn_in': w_ffn_in[li].astype(BF16),
            'w_ffn_out': w_ffn_out[li].astype(BF16),
        }
        lam_init = 0.8 - 0.6 * math.exp(-0.3 * li)

        s0_ctx = jnp.zeros((batch, 2, N_HEADS_R // 2, LANES, LANES), F32)
        y_prompt, z_ctx, s_ctx = _layer(y_prompt, batch, mod_ctx, lp, lam_init, None, s0_ctx, None,
                                        tm=1024, tq=seq_len, nbb=1)
        ks_out.append(z_ctx[:, :, TILE_K * COL:(TILE_K + 1) * COL].reshape(batch, seq_len, N_HEADS_A, 2, HD_A))
        vs_out.append(z_ctx[:, :, TILE_V * COL:(TILE_V + 1) * COL].reshape(batch, seq_len, N_HEADS_A, VD_A))
        ss_out.append(_pairs_to_state(s_ctx))

        cache = (cache_k[:, li].reshape(dec_batch, past, COL), cache_v[:, li].reshape(dec_batch, past, COL))
        s0_lat = _state_to_pairs(state_rwkv[:, li])
        y_sample, _, _ = _layer(y_sample, dec_batch, mod_lat, lp, lam_init, cache, s0_lat, rope,
                                tm=1024, tq=256, nbb=1)
    new_k = jnp.stack(ks_out, axis=1)
    new_v = jnp.stack(vs_out, axis=1)
    new_state = jnp.stack(ss_out, axis=1)
    return (y_prompt.reshape(batch, seq_len, d), y_sample, new_k, new_v, new_state)
```

```python
import functools
import math

import jax
import jax.numpy as jnp
from jax import lax
from jax.experimental import pallas as pl
from jax.experimental.pallas import tpu as pltpu

F32 = jnp.float32
BF16 = jnp.bfloat16

N_HEADS_A = 4
HD_A = 64
VD_A = 2 * HD_A
N_HEADS_R = 8
HD_R = 64
W_R = N_HEADS_R * HD_R
LORA = 64
GRID_W = 64
ROPE_THETA = 10000.0
EPS_RMS = 1e-6
EPS_GN = 64e-5

LANES = 128
VMEM_LIMIT = 48 * 1024 * 1024

CHUNK = 64
SUB = 8
COL = 512

NN = (((1,), (0,)), ((), ()))
NT = (((1,), (1,)), ((), ()))
TN = (((0,), (0,)), ((), ()))


def _dot(a, b, dims=NN):
    return lax.dot_general(a.astype(BF16), b.astype(BF16), dims, preferred_element_type=F32)


def _split2(x):
    hi = x.astype(BF16)
    lo = (x - hi.astype(F32)).astype(BF16)
    return hi, lo


def _split3(x):
    hi = x.astype(BF16)
    r1 = x - hi.astype(F32)
    mid = r1.astype(BF16)
    lo = (r1 - mid.astype(F32)).astype(BF16)
    return hi, mid, lo


def _dot_x3(a, b, dims=NN):
    ah, al = _split2(a)
    bh, bl = _split2(b)
    d = functools.partial(lax.dot_general, dimension_numbers=dims, preferred_element_type=F32)
    return d(ah, bh) + d(ah, bl) + d(al, bh)


def _sigmoid(x):
    return 1.0 / (1.0 + jnp.exp(-x))


def _idiv(x, pow2):
    assert pow2 & (pow2 - 1) == 0
    return x >> int(math.log2(pow2))


def _group_ones(n, group):
    r = _idiv(lax.broadcasted_iota(jnp.int32, (n, n), 0), group)
    c = _idiv(lax.broadcasted_iota(jnp.int32, (n, n), 1), group)
    return jnp.where(r == c, 1.0, 0.0).astype(BF16)


def _group_sum(x, ones_bd):
    outs = []
    for cb in range(x.shape[1] // LANES):
        hi, lo = _split2(x[:, cb * LANES:(cb + 1) * LANES])
        outs.append(jnp.dot(hi, ones_bd, preferred_element_type=F32) + jnp.dot(lo, ones_bd, preferred_element_type=F32))
    return outs[0] if len(outs) == 1 else jnp.concatenate(outs, axis=1)


def _cparams(sem):
    return pltpu.CompilerParams(dimension_semantics=sem, vmem_limit_bytes=VMEM_LIMIT)


def _mod_kernel(c_ref, w_ref, b_ref, o_ref):
    c = c_ref[...]
    s = c * _sigmoid(c)
    o_ref[...] = _dot_x3(s, w_ref[...]) + b_ref[...]


def _modulation(cond, ada_w, ada_b):
    rows, d = cond.shape
    n = ada_w.shape[1]
    tn = 1536
    return pl.pallas_call(
        _mod_kernel,
        grid=(n // tn,),
        in_specs=[pl.BlockSpec((rows, d), lambda j: (0, 0)),
                  pl.BlockSpec((d, tn), lambda j: (0, j)),
                  pl.BlockSpec((1, tn), lambda j: (0, j))],
        out_specs=pl.BlockSpec((rows, tn), lambda j: (0, j)),
        out_shape=jax.ShapeDtypeStruct((rows, n), F32),
        compiler_params=_cparams(("parallel",)),
        name="modulation",
    )(cond, ada_w, ada_b.reshape(1, n))


def _rope_kernel(cos_ref, sin_ref, *, tm):
    i = pl.program_id(0)
    shift = int(math.log2(GRID_W))
    t = i * tm + lax.broadcasted_iota(jnp.int32, (tm, LANES), 0)
    lane = lax.broadcasted_iota(jnp.int32, (tm, LANES), 1)
    l64 = lane & (HD_A - 1)
    nf = HD_A // 4
    f = (l64 & (nf - 1)).astype(F32)
    inv = jnp.exp(f * (-math.log(ROPE_THETA) / nf))
    pos = jnp.where(l64 < HD_A // 2, t >> shift, t & (GRID_W - 1)).astype(F32)
    ang = pos * inv
    sn = jnp.sin(ang)
    cos_ref[...] = jnp.cos(ang)
    sin_ref[...] = jnp.where((l64 & (2 * nf - 1)) < nf, -sn, sn)


def _rope_tables(t_len):
    assert GRID_W & (GRID_W - 1) == 0
    tm = 512
    return pl.pallas_call(
        functools.partial(_rope_kernel, tm=tm),
        grid=(t_len // tm,),
        out_specs=[pl.BlockSpec((tm, LANES), lambda i: (i, 0))] * 2,
        out_shape=[jax.ShapeDtypeStruct((t_len, LANES), F32)] * 2,
        compiler_params=_cparams(("parallel",)),
        name="rope_tables",
    )()


TILE_Q, TILE_K, TILE_V, TILE_RR, TILE_KR, TILE_VR, TILE_GR, TILE_LORA, TILE_GATE = 0, 1, 2, 3, 4, 5, 6, 7, 8
N_TILES = 12


def _rope_apply(y, cos, sin):
    n = y.shape[1]
    lane = lax.broadcasted_iota(jnp.int32, y.shape, 1)
    quarter = HD_A // 4
    first = (lane & (2 * quarter - 1)) < quarter
    swapped = jnp.where(first, pltpu.roll(y, n - quarter, 1), pltpu.roll(y, quarter, 1))
    reps = n // LANES
    cos_f = jnp.concatenate([cos] * reps, axis=1)
    sin_f = jnp.concatenate([sin] * reps, axis=1)
    return y * cos_f + swapped * sin_f


def _inproj_kernel(*refs, latent):
    if latent:
        x_ref, mod_ref, nw_ref, w_ref, qkw_ref, cos_ref, sin_ref, z_ref, qkv_ref, h_scr = refs
    else:
        x_ref, mod_ref, nw_ref, w_ref, qkw_ref, z_ref, qkv_ref, h_scr = refs
    j = pl.program_id(2)
    d = x_ref.shape[2]

    @pl.when(j == 0)
    def _():
        x = x_ref[0]
        ms = jnp.mean(x * x, axis=-1, keepdims=True)
        y = x * lax.rsqrt(ms + EPS_RMS) * nw_ref[...]
        sh1 = mod_ref[0, :, 0:d]
        sc1 = mod_ref[0, :, d:2 * d]
        h_scr[...] = (y * (1.0 + sc1) + sh1).astype(BF16)

    z = jnp.dot(h_scr[...], w_ref[...], preferred_element_type=F32)

    def qk_norm(zz, w_row):
        ms = _group_sum(zz * zz, _group_ones(LANES, HD_A)) * (1.0 / HD_A)
        return zz * lax.rsqrt(ms + EPS_RMS) * w_row

    def maybe_rope(y):
        return _rope_apply(y, cos_ref[...], sin_ref[...]) if latent else y

    @pl.when(j == TILE_Q)
    def _():
        y = maybe_rope(qk_norm(z, qkw_ref[0:1, :]))
        z_ref[0] = y
        qkv_ref[0] = y.astype(BF16)

    @pl.when(j == TILE_K)
    def _():
        y = qk_norm(z, qkw_ref[1:2, :])
        z_ref[0] = y
        qkv_ref[0] = maybe_rope(y).astype(BF16)

    @pl.when(j == TILE_V)
    def _():
        z_ref[0] = z
        qkv_ref[0] = z.astype(BF16)

    @pl.when(j >= TILE_GATE)
    def _():
        z_ref[0] = _sigmoid(z)

    @pl.when(jnp.logical_and(j > TILE_V, j < TILE_GATE))
    def _():
        z_ref[0] = z


def _inproj(x, mod, norm1_w, w_bf, qkw, rope, *, tm):
    nb, t_len, d = x.shape
    latent = rope is not None
    ncols = w_bf.shape[1]
    in_specs = [pl.BlockSpec((1, tm, d), lambda b, i, j: (b, i, 0)),
                pl.BlockSpec((1, 1, mod.shape[2]), lambda b, i, j: (b, 0, 0)),
                pl.BlockSpec((1, d), lambda b, i, j: (0, 0)),
                pl.BlockSpec((d, COL), lambda b, i, j: (0, j)),
                pl.BlockSpec((2, COL), lambda b, i, j: (0, 0))]
    args = [x, mod, norm1_w.reshape(1, d), w_bf, qkw]
    if latent:
        in_specs += [pl.BlockSpec((tm, LANES), lambda b, i, j: (i, 0))] * 2
        args += list(rope)
    out_specs = [pl.BlockSpec((1, tm, COL), lambda b, i, j: (b, i, j)),
                 pl.BlockSpec((1, tm, COL), lambda b, i, j: (b, i, jnp.minimum(j, TILE_V)))]
    out_shape = [jax.ShapeDtypeStruct((nb, t_len, ncols), F32),
                 jax.ShapeDtypeStruct((nb, t_len, 3 * COL), BF16)]
    return pl.pallas_call(
        functools.partial(_inproj_kernel, latent=latent),
        grid=(nb, t_len // tm, ncols // COL),
        in_specs=in_specs,
        out_specs=out_specs,
        out_shape=out_shape,
        scratch_shapes=[pltpu.VMEM((tm, d), BF16)],
        compiler_params=_cparams(("parallel", "parallel", "arbitrary")),
        name="inproj_latent" if latent else "inproj_context",
    )(*args)


KV_CHUNK = 512


def _attn_kernel(*refs, has_cache, lam_init):
    if has_cache:
        q_ref, k_ref, v_ref, ck_ref, cv_ref, lam_ref, sw_ref, o_ref = refs
    else:
        q_ref, k_ref, v_ref, lam_ref, sw_ref, o_ref = refs
    tq = q_ref.shape[1]
    t_k = k_ref.shape[1]
    q = q_ref[0].astype(F32) * (HD_A ** -0.5 * math.log2(math.e))
    lane = lax.broadcasted_iota(jnp.int32, q.shape, 1)
    qs = jnp.concatenate([jnp.where(lane < HD_A, q, 0.0), jnp.where(lane >= HD_A, q, 0.0)], axis=0).astype(BF16)

    chunks = []
    if has_cache:
        chunks.append((ck_ref[0], cv_ref[0]))
    kc = min(KV_CHUNK, t_k)
    for c in range(t_k // kc):
        chunks.append((k_ref[0, c * kc:(c + 1) * kc, :], v_ref[0, c * kc:(c + 1) * kc, :]))

    def scores(c):
        return lax.dot_general(qs, chunks[c][0], NT, preferred_element_type=F32)

    m = l = acc = None
    s_next = scores(0)
    for c, (_, v_c) in enumerate(chunks):
        s = s_next
        if c + 1 < len(chunks):
            s_next = scores(c + 1)
        mx = jnp.max(s, axis=-1, keepdims=True)
        m_new = mx if m is None else jnp.maximum(m, mx)
        p = jnp.exp2(s - m_new)
        psum = p[:, 0:LANES]
        for cb in range(1, p.shape[1] // LANES):
            psum = psum + p[:, cb * LANES:(cb + 1) * LANES]
        pv = jnp.dot(p.astype(BF16), v_c, preferred_element_type=F32)
        if m is None:
            l, acc = psum, pv
        else:
            alpha = jnp.exp2(m - m_new)
            l = alpha * l + psum
            acc = alpha * acc + pv
        m = m_new
    o = acc / jnp.sum(l, axis=-1, keepdims=True)
    lp = lam_ref[...]
    lam = (jnp.exp(jnp.sum(lp[0:1] * lp[1:2], axis=-1, keepdims=True))
           - jnp.exp(jnp.sum(lp[2:3] * lp[3:4], axis=-1, keepdims=True)) + lam_init)
    o = o[0:tq] - lam * o[tq:2 * tq]
    ms = jnp.mean(o * o, axis=-1, keepdims=True)
    o_ref[0] = o * lax.rsqrt(ms + EPS_RMS) * sw_ref[...] * (1.0 - lam_init)


def _attention(qkv, cache, lam_p, subln_w, lam_init, *, tq):
    nb, t_len, _ = qkv.shape
    cpb = COL // LANES
    in_specs = [pl.BlockSpec((1, tq, LANES), lambda b, h, i: (b, i, h)),
                pl.BlockSpec((1, t_len, LANES), lambda b, h, i: (b, 0, cpb + h)),
                pl.BlockSpec((1, t_len, LANES), lambda b, h, i: (b, 0, 2 * cpb + h))]
    args = [qkv, qkv, qkv]
    if cache is not None:
        ck, cv = cache
        past = ck.shape[1]
        in_specs += [pl.BlockSpec((1, past, LANES), lambda b, h, i: (b, 0, h))] * 2
        args += [ck, cv]
    in_specs += [pl.BlockSpec((4, HD_A), lambda b, h, i: (0, 0)),
                 pl.BlockSpec((1, VD_A), lambda b, h, i: (0, 0))]
    args += [lam_p, subln_w.reshape(1, VD_A)]
    return pl.pallas_call(
        functools.partial(_attn_kernel, has_cache=cache is not None, lam_init=lam_init),
        grid=(nb, N_HEADS_A, t_len // tq),
        in_specs=in_specs,
        out_specs=pl.BlockSpec((1, tq, LANES), lambda b, h, i: (b, i, h)),
        out_shape=jax.ShapeDtypeStruct((nb, t_len, N_HEADS_A * VD_A), F32),
        compiler_params=_cparams(("parallel", "parallel", "arbitrary")),
        name="attention_latent" if cache is not None else "attention_context",
    )(*args)


def _prep_kernel(r_ref, k_ref, la_ref, wup_ref, aup_ref, w0_ref, a0_ref, kk_ref_p, ka_ref, rk_ref,
                 kk_out, lw_out, b_out, kd_out, bs_out):
    r = r_ref[0]
    k = k_ref[0]
    la = la_ref[0]
    ones_bd = _group_ones(LANES, HD_R)
    kraw = k * kk_ref_p[...]
    nrm = jnp.sqrt(_group_sum(kraw * kraw, ones_bd))
    kk = kraw / jnp.maximum(nrm, 1e-12)
    kk_out[0] = kk
    wl = jnp.tanh(la[:, 0:2 * LORA])
    al = la[:, 2 * LORA:4 * LORA]
    lane = lax.broadcasted_iota(jnp.int32, wl.shape, 1)
    bonus = jnp.zeros_like(r)
    for z in range(2):
        sel = _idiv(lane, LORA) == z
        w = w0_ref[z:z + 1, :] + _dot_x3(jnp.where(sel, wl, 0.0), wup_ref[...])
        nw = -w
        softplus = jnp.maximum(nw, 0.0) + jnp.log(1.0 + jnp.exp(-jnp.abs(nw)))
        w_log = -softplus - 0.5
        lw_out[z, 0] = -jnp.exp(w_log)
        a = _sigmoid(a0_ref[z:z + 1, :] + _dot_x3(jnp.where(sel, al, 0.0), aup_ref[...]))
        kd = k * (1.0 + (a - 1.0) * ka_ref[...])
        b_out[z, 0] = kk * a
        kd_out[z, 0] = kd
        bonus = bonus + _group_sum(r * kd * rk_ref[...], ones_bd)
    bs_out[0] = bonus


def _rwkv_prep(z, w_lora_up, a_lora_up, w0, a0, k_k, k_a, r_k, *, tm):
    nb, t_len, _ = z.shape
    tok = lambda col: pl.BlockSpec((1, tm, COL), lambda b, i: (b, i, col))
    par = lambda rows: pl.BlockSpec((rows, W_R), lambda b, i: (0, 0))
    dir_spec = pl.BlockSpec((2, 1, tm, W_R), lambda b, i: (0, b, i, 0))
    tok_out = pl.BlockSpec((1, tm, W_R), lambda b, i: (b, i, 0))
    one = jax.ShapeDtypeStruct((nb, t_len, W_R), F32)
    two = jax.ShapeDtypeStruct((2, nb, t_len, W_R), F32)
    return pl.pallas_call(
        _prep_kernel,
        grid=(nb, t_len // tm),
        in_specs=[tok(TILE_RR), tok(TILE_KR),
                  pl.BlockSpec((1, tm, 4 * LORA), lambda b, i: (b, i, TILE_LORA * COL // (4 * LORA))),
                  par(2 * LORA), par(2 * LORA), par(2), par(2), par(1), par(1), par(1)],
        out_specs=[tok_out, dir_spec, dir_spec, dir_spec, tok_out],
        out_shape=[one, two, two, two, one],
        compiler_params=_cparams(("parallel", "parallel")),
        name="rwkv_prep",
    )(z, z, z, w_lora_up.reshape(2 * LORA, W_R), a_lora_up.reshape(2 * LORA, W_R), w0, a0,
      k_k.reshape(1, W_R), k_a.reshape(1, W_R), r_k.reshape(1, W_R))


def _pair_rows(y):
    lane = lax.broadcasted_iota(jnp.int32, y.shape, 1) & (LANES - 1)
    return jnp.concatenate([jnp.where(lane < HD_R, y, 0.0).astype(BF16),
                            jnp.where(lane >= HD_R, y, 0.0).astype(BF16)], axis=0)


def _rwkv_chunk_kernel(rf, kkf, vf, lwf, bf_, kdf, rb, kkb, vb, lwb, bb_, kdb, s0_ref,
                       yf_ref, yb_ref, sfin_ref, s_scr, *, nbb):
    c = pl.program_id(1)
    n_pairs = W_R // LANES

    @pl.when(c == 0)
    def _():
        s_scr[...] = s0_ref[...]

    C = CHUNK
    t_i = lax.broadcasted_iota(jnp.int32, (C, C), 0)
    i_i = lax.broadcasted_iota(jnp.int32, (C, C), 1)
    t2 = lax.broadcasted_iota(jnp.int32, (C, LANES), 0)
    i2 = lax.broadcasted_iota(jnp.int32, (C, LANES), 1) & (HD_R - 1)
    same_sub = _idiv(t2, SUB) == _idiv(i2, SUB)
    eye = lax.broadcasted_iota(jnp.int32, (LANES, LANES), 0) == lax.broadcasted_iota(jnp.int32, (LANES, LANES), 1)
    same_head = (_idiv(lax.broadcasted_iota(jnp.int32, (LANES, LANES), 0), HD_R)
                 == _idiv(lax.broadcasted_iota(jnp.int32, (LANES, LANES), 1), HD_R))
    masks = {}
    for rev in (False, True):
        tri = (i_i >= t_i) if rev else (i_i <= t_i)
        masks[rev] = dict(tri=jnp.where(tri, 1.0, 0.0).astype(BF16),
                          strict=(i2 > t2) if rev else (i2 < t2),
                          incl=(i2 >= t2) if rev else (i2 <= t2))

    chains = []
    for bi in range(nbb):
        for rev, (r_, kk_, v_, lw_, b_, kd_) in ((False, (rf, kkf, vf, lwf, bf_, kdf)),
                                                  (True, (rb, kkb, vb, lwb, bb_, kdb))):
            for p in range(n_pairs):
                sl = slice(p * LANES, (p + 1) * LANES)
                chains.append(dict(bi=bi, rev=rev, p=p, r=r_[bi, :, sl], kk=kk_[bi, :, sl], v=v_[bi, :, sl],
                                   lw=lw_[bi, :, sl], b=b_[bi, :, sl], kd=kd_[bi, :, sl]))

    for ch in chains:
        h, m, l = _split3(ch['lw'])
        cl = jnp.dot(masks[ch['rev']]['tri'], jnp.concatenate([h, m, l], axis=1), preferred_element_type=F32)
        ch['cl'] = cl[:, 0:LANES] + cl[:, LANES:2 * LANES] + cl[:, 2 * LANES:3 * LANES]
    for ch in chains:
        cl, lw = ch['cl'], ch['lw']
        p_in = jnp.exp(cl)
        p_inv = jnp.exp(-cl)
        p_ex = jnp.exp(cl - lw)
        ch['pc'] = jnp.exp(jnp.sum(lw, axis=0, keepdims=True))
        p_end = ch['pc'] * p_inv
        ch['at'] = -ch['kk'] * p_ex
        ch['rt'] = ch['r'] * p_in
        ch['bt'] = ch['b'] * p_inv
        ch['kt'] = ch['kd'] * p_inv
        ch['bh'] = ch['b'] * p_end
        ch['kh'] = ch['kd'] * p_end
        ch['S'] = s_scr[ch['bi'], int(ch['rev']), ch['p']]
    for ch in chains:
        g = lax.dot_general(jnp.concatenate([ch['at'], ch['rt']], axis=0).astype(BF16),
                            jnp.concatenate([_pair_rows(ch['bt']), _pair_rows(ch['kt'])], axis=0),
                            NT, preferred_element_type=F32)
        mk = masks[ch['rev']]
        ch['lab'] = jnp.where(mk['strict'], g[0:C, 0:LANES], 0.0)
        ch['lak'] = jnp.where(mk['strict'], g[0:C, LANES:2 * LANES], 0.0)
        ch['lrb'] = jnp.where(mk['incl'], g[C:2 * C, 0:LANES], 0.0)
        ch['lrk'] = jnp.where(mk['incl'], g[C:2 * C, LANES:2 * LANES], 0.0)
        ch['vbd'] = _pair_rows(ch['v'])
    for ch in chains:
        x0 = jnp.dot(jnp.concatenate([ch['at'], ch['lak']], axis=1).astype(BF16),
                     jnp.concatenate([ch['S'].astype(BF16), ch['vbd']], axis=0), preferred_element_type=F32)
        nd = jnp.where(same_sub, ch['lab'], 0.0)
        no = jnp.where(same_sub, 0.0, ch['lab'])
        ch['zc'] = jnp.concatenate([x0, no], axis=1)
        ch['nj'] = nd
    n1 = int(math.log2(SUB))
    for lev in range(n1):
        for ch in chains:
            if lev < n1 - 1:
                rr = jnp.dot(ch['nj'].astype(BF16), _pair_rows(jnp.concatenate([ch['zc'], ch['nj']], axis=1)),
                             preferred_element_type=F32)
                ch['zc'] = ch['zc'] + rr[:, 0:2 * LANES]
                ch['nj'] = rr[:, 2 * LANES:3 * LANES]
            else:
                ch['zc'] = ch['zc'] + jnp.dot(ch['nj'].astype(BF16), _pair_rows(ch['zc']),
                                              preferred_element_type=F32)
    for ch in chains:
        ch['x'] = ch['zc'][:, 0:LANES]
        ch['mj'] = ch['zc'][:, LANES:2 * LANES]
    n2 = int(math.log2(C // SUB))
    for lev in range(n2):
        for ch in chains:
            if lev < n2 - 1:
                rr = jnp.dot(ch['mj'].astype(BF16), _pair_rows(jnp.concatenate([ch['x'], ch['mj']], axis=1)),
                             preferred_element_type=F32)
                ch['x'] = ch['x'] + rr[:, 0:LANES]
                ch['mj'] = rr[:, LANES:2 * LANES]
            else:
                ch['x'] = ch['x'] + jnp.dot(ch['mj'].astype(BF16), _pair_rows(ch['x']), preferred_element_type=F32)
    for ch in chains:
        u = ch['x']
        ubd = _pair_rows(u)
        y = jnp.dot(jnp.concatenate([ch['rt'], ch['lrb'], ch['lrk']], axis=1).astype(BF16),
                    jnp.concatenate([ch['S'].astype(BF16), ubd, ch['vbd']], axis=0), preferred_element_type=F32)
        dg = jnp.where(eye, jnp.broadcast_to(ch['pc'], (LANES, LANES)), 0.0)
        s_new = lax.dot_general(jnp.concatenate([ch['bh'], ch['kh'], dg], axis=0).astype(BF16),
                                jnp.concatenate([u, ch['v'], ch['S']], axis=0).astype(BF16),
                                TN, preferred_element_type=F32)
        s_new = jnp.where(same_head, s_new, 0.0)
        sl = slice(ch['p'] * LANES, (ch['p'] + 1) * LANES)
        if ch['rev']:
            yb_ref[ch['bi'], :, sl] = y
        else:
            yf_ref[ch['bi'], :, sl] = y
        s_scr[ch['bi'], int(ch['rev']), ch['p']] = s_new

    @pl.when(c == pl.num_programs(1) - 1)
    def _():
        sfin_ref[...] = s_scr[...]


def _rwkv_scan(r_src, r_col, v_col, kk, lw, bb, kd, s0, *, nbb):
    nb, t_len, _ = kk.shape
    nc = t_len // CHUNK
    n_pairs = W_R // LANES

    def tok(col, rev):
        return pl.BlockSpec((nbb, CHUNK, W_R), (lambda g, c: (g, nc - 1 - c, col)) if rev else (lambda g, c: (g, c, col)))

    def dirs(z, rev):
        return pl.BlockSpec((None, nbb, CHUNK, W_R),
                            (lambda g, c: (z, g, nc - 1 - c, 0)) if rev else (lambda g, c: (z, g, c, 0)))

    st_spec = pl.BlockSpec((nbb, 2, n_pairs, LANES, LANES), lambda g, c: (g, 0, 0, 0, 0))
    in_specs, args = [], []
    for rev in (False, True):
        z = int(rev)
        in_specs += [tok(r_col, rev), tok(0, rev), tok(v_col, rev), dirs(z, rev), dirs(z, rev), dirs(z, rev)]
        args += [r_src, kk, r_src, lw, bb, kd]
    in_specs.append(st_spec)
    args.append(s0)
    yshape = jax.ShapeDtypeStruct((nb, t_len, W_R), F32)
    return pl.pallas_call(
        functools.partial(_rwkv_chunk_kernel, nbb=nbb),
        grid=(nb // nbb, nc),
        in_specs=in_specs,
        out_specs=[tok(0, False), tok(0, True), st_spec],
        out_shape=[yshape, yshape, jax.ShapeDtypeStruct(s0.shape, F32)],
        scratch_shapes=[pltpu.VMEM((nbb, 2, n_pairs, LANES, LANES), F32)],
        compiler_params=_cparams(("parallel", "arbitrary")),
        name="rwkv_scan",
    )(*args)


def _merge_kernel(x_ref, yf_ref, yb_ref, bs_ref, vr_ref, gr_ref, oa_ref, ga_ref, gg_ref, mod_ref,
                  lnw_ref, lnb_ref, wa_ref, wr_ref, wo_ref, n2_ref, x1_ref, h2_ref):
    d = x_ref.shape[2]
    ones_bd = _group_ones(LANES, HD_R)
    y = yf_ref[0] + yb_ref[0]
    mu = _group_sum(y, ones_bd) * (1.0 / HD_R)
    yc = y - mu
    var = _group_sum(yc * yc, ones_bd) * (1.0 / HD_R)
    yn = yc * lax.rsqrt(var + EPS_GN) * lnw_ref[...] + lnb_ref[...]
    o_r = (yn + bs_ref[0] * vr_ref[0]) * _sigmoid(gr_ref[0])
    merged = ga_ref[0] * _dot(oa_ref[0], wa_ref[...]) + gg_ref[0] * _dot(o_r, wr_ref[...])
    g1 = mod_ref[0, :, 2 * d:3 * d]
    x1 = x_ref[0] + g1 * _dot(merged, wo_ref[...])
    x1_ref[0] = x1
    sh2 = mod_ref[0, :, 3 * d:4 * d]
    sc2 = mod_ref[0, :, 4 * d:5 * d]
    ms = jnp.mean(x1 * x1, axis=-1, keepdims=True)
    h2_ref[0] = (x1 * lax.rsqrt(ms + EPS_RMS) * n2_ref[...] * (1.0 + sc2) + sh2).astype(BF16)


def _merge(x, z, yf, yb, bsum, oa, mod, ln_x_w, ln_x_b, wa_bf, wr_bf, wo_bf, norm2_w, *, tm):
    nb, t_len, d = x.shape
    tok = lambda w, col: pl.BlockSpec((1, tm, w), lambda b, i: (b, i, col))
    full = lambda a: pl.BlockSpec(a.shape, lambda b, i: (0,) * a.ndim)
    lnw, lnb, n2 = ln_x_w.reshape(1, W_R), ln_x_b.reshape(1, W_R), norm2_w.reshape(1, d)
    gate_blk = TILE_GATE * COL // d
    return pl.pallas_call(
        _merge_kernel,
        grid=(nb, t_len // tm),
        in_specs=[tok(d, 0), tok(W_R, 0), tok(W_R, 0), tok(W_R, 0), tok(COL, TILE_VR), tok(COL, TILE_GR),
                  tok(W_R, 0), tok(d, gate_blk), tok(d, gate_blk + 1),
                  pl.BlockSpec((1, 1, mod.shape[2]), lambda b, i: (b, 0, 0)),
                  full(lnw), full(lnb), full(wa_bf), full(wr_bf), full(wo_bf), full(n2)],
        out_specs=[tok(d, 0), tok(d, 0)],
        out_shape=[jax.ShapeDtypeStruct((nb, t_len, d), F32), jax.ShapeDtypeStruct((nb, t_len, d), BF16)],
        compiler_params=_cparams(("parallel", "parallel")),
        name="merge",
    )(x, yf, yb, bsum, z, z, oa, z, z, mod, lnw, lnb, wa_bf, wr_bf, wo_bf, n2)


def _ffn_kernel(h_ref, x1_ref, mod_ref, wu_ref, wg_ref, wo_ref, o_ref, acc_ref):
    f = pl.program_id(2)
    d = x1_ref.shape[2]

    @pl.when(f == 0)
    def _():
        acc_ref[...] = jnp.zeros_like(acc_ref)

    h = h_ref[0]
    u = jnp.dot(h, wu_ref[...], preferred_element_type=F32)
    g = jnp.dot(h, wg_ref[...], preferred_element_type=F32)
    a = (u * _sigmoid(u)) * g
    acc_ref[...] += jnp.dot(a.astype(BF16), wo_ref[...], preferred_element_type=F32)

    @pl.when(f == pl.num_programs(2) - 1)
    def _():
        g2 = mod_ref[0, :, 5 * d:6 * d]
        o_ref[0] = x1_ref[0] + g2 * acc_ref[...]


def _ffn(h2, x1, mod, w_in_bf, w_out_bf, *, tm, tf):
    nb, t_len, d = x1.shape
    d_ff = w_out_bf.shape[0]
    nf = d_ff // tf
    return pl.pallas_call(
        _ffn_kernel,
        grid=(nb, t_len // tm, nf),
        in_specs=[pl.BlockSpec((1, tm, d), lambda b, i, f: (b, i, 0)),
                  pl.BlockSpec((1, tm, d), lambda b, i, f: (b, i, 0)),
                  pl.BlockSpec((1, 1, mod.shape[2]), lambda b, i, f: (b, 0, 0)),
                  pl.BlockSpec((d, tf), lambda b, i, f: (0, f)),
                  pl.BlockSpec((d, tf), lambda b, i, f: (0, nf + f)),
                  pl.BlockSpec((tf, d), lambda b, i, f: (f, 0))],
        out_specs=pl.BlockSpec((1, tm, d), lambda b, i, f: (b, i, 0)),
        out_shape=jax.ShapeDtypeStruct((nb, t_len, d), F32),
        scratch_shapes=[pltpu.VMEM((tm, d), F32)],
        compiler_params=_cparams(("parallel", "parallel", "arbitrary")),
        name="ffn",
    )(h2, x1, mod, w_in_bf, w_in_bf, w_out_bf)


def _state_to_pairs(s):
    nb = s.shape[0]
    st = jnp.swapaxes(s, -1, -2).reshape(nb, 2, N_HEADS_R // 2, 2, HD_R, HD_R)
    zero = jnp.zeros_like(st[:, :, :, 0])
    top = jnp.concatenate([st[:, :, :, 0], zero], axis=-1)
    bot = jnp.concatenate([zero, st[:, :, :, 1]], axis=-1)
    return jnp.concatenate([top, bot], axis=-2)


def _pairs_to_state(sp):
    nb = sp.shape[0]
    h0 = sp[:, :, :, 0:HD_R, 0:HD_R]
    h1 = sp[:, :, :, HD_R:, HD_R:]
    st = jnp.stack([h0, h1], axis=3).reshape(nb, 2, N_HEADS_R, HD_R, HD_R)
    return jnp.swapaxes(st, -1, -2)


def _layer(x_tok, nb_seq, mod, lp, lam_init, cache, s0_pairs, rope, *, tm, tq, nbb):
    nbm, tmod, d = x_tok.shape
    t_seq = nbm * tmod // nb_seq
    z, qkv = _inproj(x_tok, mod, lp['norm1_w'], lp['w_in'], lp['qkw'], rope, tm=tm)
    zs = z.reshape(nb_seq, t_seq, z.shape[2])
    oa = _attention(qkv.reshape(nb_seq, t_seq, qkv.shape[2]), cache, lp['lam'], lp['subln_w'], lam_init, tq=tq)
    kk, lw, bb, kd, bsum = _rwkv_prep(z, lp['w_lora_up'], lp['a_lora_up'], lp['w0'], lp['a0'],
                                      lp['k_k'], lp['k_a'], lp['r_k'], tm=512)
    seq = lambda a: a.reshape(a.shape[:-3] + (nb_seq, t_seq, a.shape[-1]))
    yf, yb, s_fin = _rwkv_scan(zs, TILE_RR, TILE_VR, seq(kk), seq(lw), seq(bb), seq(kd), s0_pairs, nbb=nbb)
    tokv = lambda a: a.reshape(nbm, tmod, a.shape[-1])
    x1, h2 = _merge(x_tok, z, tokv(yf), tokv(yb), bsum, tokv(oa), mod, lp['ln_x_w'], lp['ln_x_b'],
                    lp['w_attn_br'], lp['w_rwkv_br'], lp['w_out'], lp['norm2_w'], tm=512)
    y = _ffn(h2, x1, mod, lp['w_ffn_in'], lp['w_ffn_out'], tm=512, tf=lp['w_ffn_out'].shape[0] // 2)
    return y, zs, s_fin


def kernel(x_prompt, x_sample, cache_k, cache_v, state_rwkv, c, c_ctx, ada_w, ada_b, norm1_w, norm2_w, w_in, q_norm_w, k_norm_w, lambda_q1, lambda_k1, lambda_q2, lambda_k2, subln_w, w_lora_up, w0, a_lora_up, a0, k_k, k_a, r_k, ln_x_w, ln_x_b, w_attn_br, w_rwkv_br, w_out, w_ffn_in, w_ffn_out):
    depth = ada_w.shape[0]
    batch, seq_len, d = x_prompt.shape
    dec_batch, dec_seq, _ = x_sample.shape
    past = cache_k.shape[2]
    qk_cols = N_HEADS_A * 2 * HD_A
    assert qk_cols == COL and W_R == COL and N_HEADS_A * VD_A == COL

    rope = _rope_tables(dec_seq)
    y_prompt = x_prompt.reshape(1, batch * seq_len, d)
    y_sample = x_sample
    ks_out, vs_out, ss_out = [], [], []
    cond_rows = 16
    for li in range(depth):
        cond = jnp.concatenate([c_ctx[None, :], c, jnp.zeros((cond_rows - 1 - dec_batch, d), F32)], axis=0)
        m = _modulation(cond, ada_w[li], ada_b[li])
        mod_ctx = m[0:1].reshape(1, 1, 6 * d)
        mod_lat = m[1:1 + dec_batch].reshape(dec_batch, 1, 6 * d)

        wi = w_in[li]
        o_lora = 3 * COL + 4 * W_R
        o_gate = o_lora + 4 * LORA
        w_re = jnp.concatenate([wi[:, :o_lora], wi[:, o_lora:o_gate], jnp.zeros((d, COL - 4 * LORA), F32),
                                wi[:, o_gate:]], axis=1).astype(BF16)
        assert w_re.shape[1] == N_TILES * COL
        lp = {
            'norm1_w': norm1_w[li], 'norm2_w': norm2_w[li], 'w_in': w_re,
            'qkw': jnp.stack([jnp.tile(q_norm_w[li], 2 * N_HEADS_A), jnp.tile(k_norm_w[li], 2 * N_HEADS_A)]),
            'lam': jnp.stack([lambda_q1[li], lambda_k1[li], lambda_q2[li], lambda_k2[li]]),
            'subln_w': subln_w[li], 'w_lora_up': w_lora_up[li], 'w0': w0[li], 'a_lora_up': a_lora_up[li],
            'a0': a0[li], 'k_k': k_k[li], 'k_a': k_a[li], 'r_k': r_k[li], 'ln_x_w': ln_x_w[li],
            'ln_x_b': ln_x_b[li], 'w_attn_br': w_attn_br[li].astype(BF16), 'w_rwkv_br': w_rwkv_br[li].astype(BF16),
            'w_out': w_out[li].astype(BF16), 'w_ffn_in': w_ffn_in[li].astype(BF16),
            'w_ffn_out': w_ffn_out[li].astype(BF16),
        }
        lam_init = 0.8 - 0.6 * math.exp(-0.3 * li)

        s0_ctx = jnp.zeros((batch, 2, N_HEADS_R // 2, LANES, LANES), F32)
        y_prompt, z_ctx, s_ctx = _layer(y_prompt, batch, mod_ctx, lp, lam_init, None, s0_ctx, None,
                                        tm=1024, tq=seq_len, nbb=2)
        ks_out.append(z_ctx[:, :, TILE_K * COL:(TILE_K + 1) * COL].reshape(batch, seq_len, N_HEADS_A, 2, HD_A))
        vs_out.append(z_ctx[:, :, TILE_V * COL:(TILE_V + 1) * COL].reshape(batch, seq_len, N_HEADS_A, VD_A))
        ss_out.append(_pairs_to_state(s_ctx))

        cache = (cache_k[:, li].reshape(dec_batch, past, COL).astype(BF16),
                 cache_v[:, li].reshape(dec_batch, past, COL).astype(BF16))
        s0_lat = _state_to_pairs(state_rwkv[:, li])
        y_sample, _, _ = _layer(y_sample, dec_batch, mod_lat, lp, lam_init, cache, s0_lat, rope,
                                tm=1024, tq=512, nbb=2)
    new_k = jnp.stack(ks_out, axis=1)
    new_v = jnp.stack(vs_out, axis=1)
    new_state = jnp.stack(ss_out, axis=1)
    return (y_prompt.reshape(batch, seq_len, d), y_sample, new_k, new_v, new_state)
```

```python
import functools
import math

import jax
import jax.numpy as jnp
from jax import lax
from jax.experimental import pallas as pl
from jax.experimental.pallas import tpu as pltpu

F32 = jnp.float32
BF16 = jnp.bfloat16

N_HEADS_A = 4
HD_A = 64
VD_A = 2 * HD_A
N_HEADS_R = 8
HD_R = 64
W_R = N_HEADS_R * HD_R
LORA = 64
GRID_W = 64
ROPE_THETA = 10000.0
EPS_RMS = 1e-6
EPS_GN = 64e-5

LANES = 128
VMEM_LIMIT = 48 * 1024 * 1024
INPROJ_VMEM_LIMIT = 58 * 1024 * 1024

CHUNK = 64
SUB = 8
COL = 512

NN = (((1,), (0,)), ((), ()))
NT = (((1,), (1,)), ((), ()))
TN = (((0,), (0,)), ((), ()))


def _dot(a, b, dims=NN):
    return lax.dot_general(a.astype(BF16), b.astype(BF16), dims, preferred_element_type=F32)


def _split2(x):
    hi = x.astype(BF16)
    lo = (x - hi.astype(F32)).astype(BF16)
    return hi, lo


def _split3(x):
    hi = x.astype(BF16)
    r1 = x - hi.astype(F32)
    mid = r1.astype(BF16)
    lo = (r1 - mid.astype(F32)).astype(BF16)
    return hi, mid, lo


def _dot_x3(a, b, dims=NN):
    ah, al = _split2(a)
    bh, bl = _split2(b)
    d = functools.partial(lax.dot_general, dimension_numbers=dims, preferred_element_type=F32)
    return d(ah, bh) + d(ah, bl) + d(al, bh)


def _sigmoid(x):
    return 1.0 / (1.0 + jnp.exp(-x))


def _idiv(x, pow2):
    assert pow2 & (pow2 - 1) == 0
    return x >> int(math.log2(pow2))


def _group_ones(n, group):
    r = _idiv(lax.broadcasted_iota(jnp.int32, (n, n), 0), group)
    c = _idiv(lax.broadcasted_iota(jnp.int32, (n, n), 1), group)
    return jnp.where(r == c, 1.0, 0.0).astype(BF16)


def _group_sum(x, ones_bd):
    outs = []
    for cb in range(x.shape[1] // LANES):
        hi, lo = _split2(x[:, cb * LANES:(cb + 1) * LANES])
        outs.append(jnp.dot(hi, ones_bd, preferred_element_type=F32) + jnp.dot(lo, ones_bd, preferred_element_type=F32))
    return outs[0] if len(outs) == 1 else jnp.concatenate(outs, axis=1)


def _cparams(sem):
    return pltpu.CompilerParams(dimension_semantics=sem, vmem_limit_bytes=VMEM_LIMIT)


def _mod_kernel(c_ref, w_ref, b_ref, o_ref):
    c = c_ref[...]
    s = c * _sigmoid(c)
    o_ref[...] = _dot_x3(s, w_ref[...]) + b_ref[...]


def _modulation(cond, ada_w, ada_b):
    rows, d = cond.shape
    n = ada_w.shape[1]
    tn = 1536
    return pl.pallas_call(
        _mod_kernel,
        grid=(n // tn,),
        in_specs=[pl.BlockSpec((rows, d), lambda j: (0, 0)),
                  pl.BlockSpec((d, tn), lambda j: (0, j)),
                  pl.BlockSpec((1, tn), lambda j: (0, j))],
        out_specs=pl.BlockSpec((rows, tn), lambda j: (0, j)),
        out_shape=jax.ShapeDtypeStruct((rows, n), F32),
        compiler_params=_cparams(("parallel",)),
        name="modulation",
    )(cond, ada_w, ada_b.reshape(1, n))


def _rope_kernel(cos_ref, sin_ref, *, tm):
    i = pl.program_id(0)
    shift = int(math.log2(GRID_W))
    t = i * tm + lax.broadcasted_iota(jnp.int32, (tm, LANES), 0)
    lane = lax.broadcasted_iota(jnp.int32, (tm, LANES), 1)
    l64 = lane & (HD_A - 1)
    nf = HD_A // 4
    f = (l64 & (nf - 1)).astype(F32)
    inv = jnp.exp(f * (-math.log(ROPE_THETA) / nf))
    pos = jnp.where(l64 < HD_A // 2, t >> shift, t & (GRID_W - 1)).astype(F32)
    ang = pos * inv
    sn = jnp.sin(ang)
    cos_ref[...] = jnp.cos(ang)
    sin_ref[...] = jnp.where((l64 & (2 * nf - 1)) < nf, -sn, sn)


def _rope_tables(t_len):
    assert GRID_W & (GRID_W - 1) == 0
    tm = 512
    return pl.pallas_call(
        functools.partial(_rope_kernel, tm=tm),
        grid=(t_len // tm,),
        out_specs=[pl.BlockSpec((tm, LANES), lambda i: (i, 0))] * 2,
        out_shape=[jax.ShapeDtypeStruct((t_len, LANES), F32)] * 2,
        compiler_params=_cparams(("parallel",)),
        name="rope_tables",
    )()


TILE_Q, TILE_K, TILE_V, TILE_RR, TILE_KR, TILE_VR, TILE_GR, TILE_LORA, TILE_GATE = 0, 1, 2, 3, 4, 5, 6, 7, 8
N_TILES = 12
ZR_RR, ZR_KR, ZR_VR, ZR_GR, ZR_LORA = 0, 1, 2, 3, 4
N_ZR = 5
N_GATE = N_TILES - TILE_GATE
Q_SCALE = HD_A ** -0.5 * math.log2(math.e)


def _rope_apply(y, cos, sin):
    n = y.shape[1]
    lane = lax.broadcasted_iota(jnp.int32, y.shape, 1)
    quarter = HD_A // 4
    first = (lane & (2 * quarter - 1)) < quarter
    swapped = jnp.where(first, pltpu.roll(y, n - quarter, 1), pltpu.roll(y, quarter, 1))
    reps = n // LANES
    cos_f = jnp.concatenate([cos] * reps, axis=1)
    sin_f = jnp.concatenate([sin] * reps, axis=1)
    return y * cos_f + swapped * sin_f


def _inproj_kernel(*refs, latent):
    if latent:
        x_ref, mod_ref, nw_ref, w_ref, qkw_ref, cos_ref, sin_ref, qkv_ref, zr_ref, gate_ref = refs
    else:
        x_ref, mod_ref, nw_ref, w_ref, qkw_ref, qkv_ref, zr_ref, gate_ref, kv_ref = refs
    d = x_ref.shape[2]
    x = x_ref[0]
    ms = jnp.mean(x * x, axis=-1, keepdims=True)
    y = x * lax.rsqrt(ms + EPS_RMS) * nw_ref[...]
    h = (y * (1.0 + mod_ref[0, :, d:2 * d]) + mod_ref[0, :, 0:d]).astype(BF16)

    def tile(j):
        return jnp.dot(h, w_ref[:, j * COL:(j + 1) * COL], preferred_element_type=F32)

    def qk_norm(zz, w_row):
        gms = _group_sum(zz * zz, _group_ones(LANES, HD_A)) * (1.0 / HD_A)
        return zz * lax.rsqrt(gms + EPS_RMS) * w_row

    def maybe_rope(v):
        return _rope_apply(v, cos_ref[...], sin_ref[...]) if latent else v

    def col(j):
        return slice(j * COL, (j + 1) * COL)

    z_next = tile(0)
    for j in range(N_TILES):
        z = z_next
        if j + 1 < N_TILES:
            z_next = tile(j + 1)
        if j == TILE_Q:
            qkv_ref[0, :, col(0)] = (maybe_rope(qk_norm(z, qkw_ref[0:1, :])) * Q_SCALE).astype(BF16)
        elif j == TILE_K:
            kn = qk_norm(z, qkw_ref[1:2, :])
            qkv_ref[0, :, col(1)] = maybe_rope(kn).astype(BF16)
            if not latent:
                kv_ref[0, :, col(0)] = kn
        elif j == TILE_V:
            qkv_ref[0, :, col(2)] = z.astype(BF16)
            if not latent:
                kv_ref[0, :, col(1)] = z
        elif j < TILE_GATE:
            zr_ref[0, :, col(j - TILE_RR)] = z
        else:
            gate_ref[0, :, col(j - TILE_GATE)] = _sigmoid(z)


def _inproj(x, mod, norm1_w, w_bf, qkw, rope, *, tm):
    nb, t_len, d = x.shape
    latent = rope is not None
    ncols = w_bf.shape[1]
    in_specs = [pl.BlockSpec((1, tm, d), lambda b, i: (b, i, 0)),
                pl.BlockSpec((1, 1, mod.shape[2]), lambda b, i: (b, 0, 0)),
                pl.BlockSpec((1, d), lambda b, i: (0, 0)),
                pl.BlockSpec((d, ncols), lambda b, i: (0, 0), pipeline_mode=pl.Buffered(1)),
                pl.BlockSpec((2, COL), lambda b, i: (0, 0))]
    args = [x, mod, norm1_w.reshape(1, d), w_bf, qkw]
    if latent:
        in_specs += [pl.BlockSpec((tm, LANES), lambda b, i: (i, 0))] * 2
        args += list(rope)
    widths = [3, N_ZR, N_GATE] + ([] if latent else [2])
    dtypes = [BF16, F32, F32] + ([] if latent else [F32])
    out_specs = [pl.BlockSpec((1, tm, n * COL), lambda b, i: (b, i, 0)) for n in widths]
    out_shape = [jax.ShapeDtypeStruct((nb, t_len, n * COL), dt) for n, dt in zip(widths, dtypes)]
    return pl.pallas_call(
        functools.partial(_inproj_kernel, latent=latent),
        grid=(nb, t_len // tm),
        in_specs=in_specs,
        out_specs=out_specs,
        out_shape=out_shape,
        compiler_params=pltpu.CompilerParams(dimension_semantics=("parallel", "parallel"),
                                             vmem_limit_bytes=INPROJ_VMEM_LIMIT),
        name="inproj_latent" if latent else "inproj_context",
    )(*args)


KV_CHUNK = 512


def _attn_kernel(*refs, has_cache, lam_init):
    if has_cache:
        q_ref, k_ref, v_ref, ck_ref, cv_ref, lam_ref, sw_ref, o_ref = refs
    else:
        q_ref, k_ref, v_ref, lam_ref, sw_ref, o_ref = refs
    tq = q_ref.shape[1]
    t_k = k_ref.shape[1]
    q = q_ref[0].astype(F32)
    lane = lax.broadcasted_iota(jnp.int32, q.shape, 1)
    qs = jnp.concatenate([jnp.where(lane < HD_A, q, 0.0), jnp.where(lane >= HD_A, q, 0.0)], axis=0).astype(BF16)

    chunks = []
    if has_cache:
        chunks.append((ck_ref[0], cv_ref[0]))
    kc = min(KV_CHUNK, t_k)
    for c in range(t_k // kc):
        chunks.append((k_ref[0, c * kc:(c + 1) * kc, :], v_ref[0, c * kc:(c + 1) * kc, :]))

    def scores_t(c):
        return lax.dot_general(chunks[c][0], qs, NT, preferred_element_type=F32)

    def fold8(x, op):
        parts = x.reshape(x.shape[0] // 8, 8, x.shape[1])
        return op(parts, axis=0)

    m = l = acc = None
    s_next = scores_t(0)
    for c, (_, v_c) in enumerate(chunks):
        s = s_next
        if c + 1 < len(chunks):
            s_next = scores_t(c + 1)
        mx = jnp.max(fold8(s, jnp.max), axis=0, keepdims=True)
        m_new = mx if m is None else jnp.maximum(m, mx)
        p = jnp.exp2(s - m_new)
        psum = fold8(p, jnp.sum)
        pv = lax.dot_general(v_c, p.astype(BF16), TN, preferred_element_type=F32)
        if m is None:
            l, acc = psum, pv
        else:
            alpha = jnp.exp2(m - m_new)
            l = alpha * l + psum
            acc = alpha * acc + pv
        m = m_new
    o = acc / jnp.sum(l, axis=0, keepdims=True)
    lp = lam_ref[...]
    lam = (jnp.exp(jnp.sum(lp[0:1] * lp[1:2], axis=-1, keepdims=True))
           - jnp.exp(jnp.sum(lp[2:3] * lp[3:4], axis=-1, keepdims=True)) + lam_init)
    o = (o[:, 0:tq] - lam * o[:, tq:2 * tq]).T
    ms = jnp.mean(o * o, axis=-1, keepdims=True)
    o_ref[0] = (o * lax.rsqrt(ms + EPS_RMS) * sw_ref[...] * (1.0 - lam_init)).astype(o_ref.dtype)


def _attention(qkv, cache, lam_p, subln_w, lam_init, *, tq):
    nb, t_len, _ = qkv.shape
    cpb = COL // LANES
    in_specs = [pl.BlockSpec((1, tq, LANES), lambda b, h, i: (b, i, h)),
                pl.BlockSpec((1, t_len, LANES), lambda b, h, i: (b, 0, cpb + h)),
                pl.BlockSpec((1, t_len, LANES), lambda b, h, i: (b, 0, 2 * cpb + h))]
    args = [qkv, qkv, qkv]
    if cache is not None:
        ck, cv = cache
        past = ck.shape[1]
        in_specs += [pl.BlockSpec((1, past, LANES), lambda b, h, i: (b, 0, h))] * 2
        args += [ck, cv]
    in_specs += [pl.BlockSpec((4, HD_A), lambda b, h, i: (0, 0)),
                 pl.BlockSpec((1, VD_A), lambda b, h, i: (0, 0))]
    args += [lam_p, subln_w.reshape(1, VD_A)]
    return pl.pallas_call(
        functools.partial(_attn_kernel, has_cache=cache is not None, lam_init=lam_init),
        grid=(nb, N_HEADS_A, t_len // tq),
        in_specs=in_specs,
        out_specs=pl.BlockSpec((1, tq, LANES), lambda b, h, i: (b, i, h)),
        out_shape=jax.ShapeDtypeStruct((nb, t_len, N_HEADS_A * VD_A), BF16),
        compiler_params=_cparams(("parallel", "parallel", "arbitrary")),
        name="attention_latent" if cache is not None else "attention_context",
    )(*args)


def _prep_kernel(r_ref, k_ref, la_ref, wup_ref, aup_ref, w0_ref, a0_ref, kk_ref_p, ka_ref, rk_ref,
                 kk_out, lw_out, b_out, kd_out, bs_out):
    r = r_ref[0]
    k = k_ref[0]
    la = la_ref[0]
    ones_bd = _group_ones(LANES, HD_R)
    kraw = k * kk_ref_p[...]
    nrm = jnp.sqrt(_group_sum(kraw * kraw, ones_bd))
    kk = kraw / jnp.maximum(nrm, 1e-12)
    kk_out[0] = kk
    wl = jnp.tanh(la[:, 0:2 * LORA])
    al = la[:, 2 * LORA:4 * LORA]
    lane = lax.broadcasted_iota(jnp.int32, wl.shape, 1)
    bonus = jnp.zeros_like(r)
    for z in range(2):
        sel = _idiv(lane, LORA) == z
        w = w0_ref[z:z + 1, :] + _dot_x3(jnp.where(sel, wl, 0.0), wup_ref[...])
        nw = -w
        softplus = jnp.maximum(nw, 0.0) + jnp.log(1.0 + jnp.exp(-jnp.abs(nw)))
        w_log = -softplus - 0.5
        lw_out[z, 0] = -jnp.exp(w_log)
        a = _sigmoid(a0_ref[z:z + 1, :] + _dot_x3(jnp.where(sel, al, 0.0), aup_ref[...]))
        kd = k * (1.0 + (a - 1.0) * ka_ref[...])
        b_out[z, 0] = kk * a
        kd_out[z, 0] = kd
        bonus = bonus + _group_sum(r * kd * rk_ref[...], ones_bd)
    bs_out[0] = bonus


def _rwkv_prep(z, w_lora_up, a_lora_up, w0, a0, k_k, k_a, r_k, *, tm):
    nb, t_len, _ = z.shape
    tok = lambda col: pl.BlockSpec((1, tm, COL), lambda b, i: (b, i, col))
    par = lambda rows: pl.BlockSpec((rows, W_R), lambda b, i: (0, 0))
    dir_spec = pl.BlockSpec((2, 1, tm, W_R), lambda b, i: (0, b, i, 0))
    tok_out = pl.BlockSpec((1, tm, W_R), lambda b, i: (b, i, 0))
    one = jax.ShapeDtypeStruct((nb, t_len, W_R), F32)
    two = jax.ShapeDtypeStruct((2, nb, t_len, W_R), F32)
    return pl.pallas_call(
        _prep_kernel,
        grid=(nb, t_len // tm),
        in_specs=[tok(ZR_RR), tok(ZR_KR),
                  pl.BlockSpec((1, tm, 4 * LORA), lambda b, i: (b, i, ZR_LORA * COL // (4 * LORA))),
                  par(2 * LORA), par(2 * LORA), par(2), par(2), par(1), par(1), par(1)],
        out_specs=[tok_out, dir_spec, dir_spec, dir_spec, tok_out],
        out_shape=[one, two, two, two, one],
        compiler_params=_cparams(("parallel", "parallel")),
        name="rwkv_prep",
    )(z, z, z, w_lora_up.reshape(2 * LORA, W_R), a_lora_up.reshape(2 * LORA, W_R), w0, a0,
      k_k.reshape(1, W_R), k_a.reshape(1, W_R), r_k.reshape(1, W_R))


def _pair_rows(y):
    lane = lax.broadcasted_iota(jnp.int32, y.shape, 1) & (LANES - 1)
    return jnp.concatenate([jnp.where(lane < HD_R, y, 0.0).astype(BF16),
                            jnp.where(lane >= HD_R, y, 0.0).astype(BF16)], axis=0)


def _rwkv_chunk_kernel(rf, kkf, vf, lwf, bf_, kdf, rb, kkb, vb, lwb, bb_, kdb, s0_ref,
                       yf_ref, yb_ref, sfin_ref, s_scr, *, nbb):
    c = pl.program_id(1)
    n_pairs = W_R // LANES

    @pl.when(c == 0)
    def _():
        s_scr[...] = s0_ref[...]

    C = CHUNK
    t_i = lax.broadcasted_iota(jnp.int32, (C, C), 0)
    i_i = lax.broadcasted_iota(jnp.int32, (C, C), 1)
    t2 = lax.broadcasted_iota(jnp.int32, (C, LANES), 0)
    i2 = lax.broadcasted_iota(jnp.int32, (C, LANES), 1) & (HD_R - 1)
    same_sub = _idiv(t2, SUB) == _idiv(i2, SUB)
    eye = lax.broadcasted_iota(jnp.int32, (LANES, LANES), 0) == lax.broadcasted_iota(jnp.int32, (LANES, LANES), 1)
    same_head = (_idiv(lax.broadcasted_iota(jnp.int32, (LANES, LANES), 0), HD_R)
                 == _idiv(lax.broadcasted_iota(jnp.int32, (LANES, LANES), 1), HD_R))
    masks = {}
    for rev in (False, True):
        tri = (i_i >= t_i) if rev else (i_i <= t_i)
        masks[rev] = dict(tri=jnp.where(tri, 1.0, 0.0).astype(BF16),
                          strict=(i2 > t2) if rev else (i2 < t2),
                          incl=(i2 >= t2) if rev else (i2 <= t2))

    chains = []
    for bi in range(nbb):
        for rev, (r_, kk_, v_, lw_, b_, kd_) in ((False, (rf, kkf, vf, lwf, bf_, kdf)),
                                                  (True, (rb, kkb, vb, lwb, bb_, kdb))):
            for p in range(n_pairs):
                sl = slice(p * LANES, (p + 1) * LANES)
                chains.append(dict(bi=bi, rev=rev, p=p, r=r_[bi, :, sl], kk=kk_[bi, :, sl], v=v_[bi, :, sl],
                                   lw=lw_[bi, :, sl], b=b_[bi, :, sl], kd=kd_[bi, :, sl]))

    for ch in chains:
        h, m, l = _split3(ch['lw'])
        cl = jnp.dot(masks[ch['rev']]['tri'], jnp.concatenate([h, m, l], axis=1), preferred_element_type=F32)
        ch['cl'] = cl[:, 0:LANES] + cl[:, LANES:2 * LANES] + cl[:, 2 * LANES:3 * LANES]
    for ch in chains:
        cl, lw = ch['cl'], ch['lw']
        p_in = jnp.exp(cl)
        p_inv = jnp.exp(-cl)
        p_ex = jnp.exp(cl - lw)
        ch['pc'] = jnp.exp(jnp.sum(lw, axis=0, keepdims=True))
        p_end = ch['pc'] * p_inv
        ch['at'] = -ch['kk'] * p_ex
        ch['rt'] = ch['r'] * p_in
        ch['bt'] = ch['b'] * p_inv
        ch['kt'] = ch['kd'] * p_inv
        ch['bh'] = ch['b'] * p_end
        ch['kh'] = ch['kd'] * p_end
        ch['S'] = s_scr[ch['bi'], int(ch['rev']), ch['p']]
    for ch in chains:
        g = lax.dot_general(jnp.concatenate([ch['at'], ch['rt']], axis=0).astype(BF16),
                            jnp.concatenate([_pair_rows(ch['bt']), _pair_rows(ch['kt'])], axis=0),
                            NT, preferred_element_type=F32)
        mk = masks[ch['rev']]
        ch['lab'] = jnp.where(mk['strict'], g[0:C, 0:LANES], 0.0)
        ch['lak'] = jnp.where(mk['strict'], g[0:C, LANES:2 * LANES], 0.0)
        ch['lrb'] = jnp.where(mk['incl'], g[C:2 * C, 0:LANES], 0.0)
        ch['lrk'] = jnp.where(mk['incl'], g[C:2 * C, LANES:2 * LANES], 0.0)
        ch['vbd'] = _pair_rows(ch['v'])
    for ch in chains:
        x0 = jnp.dot(jnp.concatenate([ch['at'], ch['lak']], axis=1).astype(BF16),
                     jnp.concatenate([ch['S'].astype(BF16), ch['vbd']], axis=0), preferred_element_type=F32)
        nd = jnp.where(same_sub, ch['lab'], 0.0)
        no = jnp.where(same_sub, 0.0, ch['lab'])
        ch['zc'] = jnp.concatenate([x0, no], axis=1)
        ch['nj'] = nd
    n1 = int(math.log2(SUB))
    for lev in range(n1):
        for ch in chains:
            if lev < n1 - 1:
                rr = jnp.dot(ch['nj'].astype(BF16), _pair_rows(jnp.concatenate([ch['zc'], ch['nj']], axis=1)),
                             preferred_element_type=F32)
                ch['zc'] = ch['zc'] + rr[:, 0:2 * LANES]
                ch['nj'] = rr[:, 2 * LANES:3 * LANES]
            else:
                ch['zc'] = ch['zc'] + jnp.dot(ch['nj'].astype(BF16), _pair_rows(ch['zc']),
                                              preferred_element_type=F32)
    for ch in chains:
        ch['x'] = ch['zc'][:, 0:LANES]
        ch['mj'] = ch['zc'][:, LANES:2 * LANES]
    n2 = int(math.log2(C // SUB))
    for lev in range(n2):
        for ch in chains:
            if lev < n2 - 1:
                rr = jnp.dot(ch['mj'].astype(BF16), _pair_rows(jnp.concatenate([ch['x'], ch['mj']], axis=1)),
                             preferred_element_type=F32)
                ch['x'] = ch['x'] + rr[:, 0:LANES]
                ch['mj'] = rr[:, LANES:2 * LANES]
            else:
                ch['x'] = ch['x'] + jnp.dot(ch['mj'].astype(BF16), _pair_rows(ch['x']), preferred_element_type=F32)
    for ch in chains:
        u = ch['x']
        ubd = _pair_rows(u)
        y = jnp.dot(jnp.concatenate([ch['rt'], ch['lrb'], ch['lrk']], axis=1).astype(BF16),
                    jnp.concatenate([ch['S'].astype(BF16), ubd, ch['vbd']], axis=0), preferred_element_type=F32)
        dg = jnp.where(eye, jnp.broadcast_to(ch['pc'], (LANES, LANES)), 0.0)
        s_new = lax.dot_general(jnp.concatenate([ch['bh'], ch['kh'], dg], axis=0).astype(BF16),
                                jnp.concatenate([u, ch['v'], ch['S']], axis=0).astype(BF16),
                                TN, preferred_element_type=F32)
        s_new = jnp.where(same_head, s_new, 0.0)
        sl = slice(ch['p'] * LANES, (ch['p'] + 1) * LANES)
        if ch['rev']:
            yb_ref[ch['bi'], :, sl] = y
        else:
            yf_ref[ch['bi'], :, sl] = y
        s_scr[ch['bi'], int(ch['rev']), ch['p']] = s_new

    @pl.when(c == pl.num_programs(1) - 1)
    def _():
        sfin_ref[...] = s_scr[...]


def _rwkv_scan(r_src, r_col, v_col, kk, lw, bb, kd, s0, *, nbb):
    nb, t_len, _ = kk.shape
    nc = t_len // CHUNK
    n_pairs = W_R // LANES

    def tok(col, rev):
        return pl.BlockSpec((nbb, CHUNK, W_R), (lambda g, c: (g, nc - 1 - c, col)) if rev else (lambda g, c: (g, c, col)))

    def dirs(z, rev):
        return pl.BlockSpec((None, nbb, CHUNK, W_R),
                            (lambda g, c: (z, g, nc - 1 - c, 0)) if rev else (lambda g, c: (z, g, c, 0)))

    st_spec = pl.BlockSpec((nbb, 2, n_pairs, LANES, LANES), lambda g, c: (g, 0, 0, 0, 0))
    in_specs, args = [], []
    for rev in (False, True):
        z = int(rev)
        in_specs += [tok(r_col, rev), tok(0, rev), tok(v_col, rev), dirs(z, rev), dirs(z, rev), dirs(z, rev)]
        args += [r_src, kk, r_src, lw, bb, kd]
    in_specs.append(st_spec)
    args.append(s0)
    yshape = jax.ShapeDtypeStruct((nb, t_len, W_R), F32)
    return pl.pallas_call(
        functools.partial(_rwkv_chunk_kernel, nbb=nbb),
        grid=(nb // nbb, nc),
        in_specs=in_specs,
        out_specs=[tok(0, False), tok(0, True), st_spec],
        out_shape=[yshape, yshape, jax.ShapeDtypeStruct(s0.shape, F32)],
        scratch_shapes=[pltpu.VMEM((nbb, 2, n_pairs, LANES, LANES), F32)],
        compiler_params=_cparams(("parallel", "arbitrary")),
        name="rwkv_scan",
    )(*args)


def _merge_kernel(x_ref, yf_ref, yb_ref, bs_ref, vr_ref, gr_ref, oa_ref, ga_ref, gg_ref, mod_ref,
                  lnw_ref, lnb_ref, wa_ref, wr_ref, wo_ref, n2_ref, x1_ref, h2_ref):
    d = x_ref.shape[2]
    ones_bd = _group_ones(LANES, HD_R)
    y = yf_ref[0] + yb_ref[0]
    mu = _group_sum(y, ones_bd) * (1.0 / HD_R)
    yc = y - mu
    var = _group_sum(yc * yc, ones_bd) * (1.0 / HD_R)
    yn = yc * lax.rsqrt(var + EPS_GN) * lnw_ref[...] + lnb_ref[...]
    o_r = (yn + bs_ref[0] * vr_ref[0]) * _sigmoid(gr_ref[0])
    merged = ga_ref[0] * _dot(oa_ref[0], wa_ref[...]) + gg_ref[0] * _dot(o_r, wr_ref[...])
    g1 = mod_ref[0, :, 2 * d:3 * d]
    x1 = x_ref[0] + g1 * _dot(merged, wo_ref[...])
    x1_ref[0] = x1
    sh2 = mod_ref[0, :, 3 * d:4 * d]
    sc2 = mod_ref[0, :, 4 * d:5 * d]
    ms = jnp.mean(x1 * x1, axis=-1, keepdims=True)
    h2_ref[0] = (x1 * lax.rsqrt(ms + EPS_RMS) * n2_ref[...] * (1.0 + sc2) + sh2).astype(BF16)


def _merge(x, zr, gates, yf, yb, bsum, oa, mod, ln_x_w, ln_x_b, wa_bf, wr_bf, wo_bf, norm2_w, *, tm):
    nb, t_len, d = x.shape
    tok = lambda w, col: pl.BlockSpec((1, tm, w), lambda b, i: (b, i, col))
    full = lambda a: pl.BlockSpec(a.shape, lambda b, i: (0,) * a.ndim)
    lnw, lnb, n2 = ln_x_w.reshape(1, W_R), ln_x_b.reshape(1, W_R), norm2_w.reshape(1, d)
    return pl.pallas_call(
        _merge_kernel,
        grid=(nb, t_len // tm),
        in_specs=[tok(d, 0), tok(W_R, 0), tok(W_R, 0), tok(W_R, 0), tok(COL, ZR_VR), tok(COL, ZR_GR),
                  tok(W_R, 0), tok(d, 0), tok(d, 1),
                  pl.BlockSpec((1, 1, mod.shape[2]), lambda b, i: (b, 0, 0)),
                  full(lnw), full(lnb), full(wa_bf), full(wr_bf), full(wo_bf), full(n2)],
        out_specs=[tok(d, 0), tok(d, 0)],
        out_shape=[jax.ShapeDtypeStruct((nb, t_len, d), F32), jax.ShapeDtypeStruct((nb, t_len, d), BF16)],
        compiler_params=_cparams(("parallel", "parallel")),
        name="merge",
    )(x, yf, yb, bsum, zr, zr, oa, gates, gates, mod, lnw, lnb, wa_bf, wr_bf, wo_bf, n2)


def _ffn_kernel(h_ref, x1_ref, mod_ref, wi_ref, wo_ref, o_ref, *, n_chunks):
    d = x1_ref.shape[2]
    d_ff = wo_ref.shape[0]
    tf = d_ff // n_chunks
    h = h_ref[0]

    def up(c):
        u = jnp.dot(h, wi_ref[:, c * tf:(c + 1) * tf], preferred_element_type=F32)
        g = jnp.dot(h, wi_ref[:, d_ff + c * tf:d_ff + (c + 1) * tf], preferred_element_type=F32)
        return u, g

    acc = None
    nxt = up(0)
    for c in range(n_chunks):
        u, g = nxt
        if c + 1 < n_chunks:
            nxt = up(c + 1)
        a = ((u * _sigmoid(u)) * g).astype(BF16)
        part = jnp.dot(a, wo_ref[c * tf:(c + 1) * tf, :], preferred_element_type=F32)
        acc = part if acc is None else acc + part
    o_ref[0] = x1_ref[0] + mod_ref[0, :, 5 * d:6 * d] * acc


def _ffn(h2, x1, mod, w_in_bf, w_out_bf, *, tm, n_chunks):
    nb, t_len, d = x1.shape
    resident = lambda a: pl.BlockSpec(a.shape, lambda b, i: (0, 0), pipeline_mode=pl.Buffered(1))
    tok = pl.BlockSpec((1, tm, d), lambda b, i: (b, i, 0))
    return pl.pallas_call(
        functools.partial(_ffn_kernel, n_chunks=n_chunks),
        grid=(nb, t_len // tm),
        in_specs=[tok, tok, pl.BlockSpec((1, 1, mod.shape[2]), lambda b, i: (b, 0, 0)),
                  resident(w_in_bf), resident(w_out_bf)],
        out_specs=tok,
        out_shape=jax.ShapeDtypeStruct((nb, t_len, d), F32),
        compiler_params=pltpu.CompilerParams(dimension_semantics=("parallel", "parallel"),
                                             vmem_limit_bytes=INPROJ_VMEM_LIMIT),
        name="ffn",
    )(h2, x1, mod, w_in_bf, w_out_bf)


def _state_to_pairs(s):
    nb = s.shape[0]
    st = jnp.swapaxes(s, -1, -2).reshape(nb, 2, N_HEADS_R // 2, 2, HD_R, HD_R)
    zero = jnp.zeros_like(st[:, :, :, 0])
    top = jnp.concatenate([st[:, :, :, 0], zero], axis=-1)
    bot = jnp.concatenate([zero, st[:, :, :, 1]], axis=-1)
    return jnp.concatenate([top, bot], axis=-2)


def _pairs_to_state(sp):
    nb = sp.shape[0]
    h0 = sp[:, :, :, 0:HD_R, 0:HD_R]
    h1 = sp[:, :, :, HD_R:, HD_R:]
    st = jnp.stack([h0, h1], axis=3).reshape(nb, 2, N_HEADS_R, HD_R, HD_R)
    return jnp.swapaxes(st, -1, -2)


def _layer(x_tok, nb_seq, mod, lp, lam_init, cache, s0_pairs, rope, *, tm, tq, nbb):
    nbm, tmod, d = x_tok.shape
    t_seq = nbm * tmod // nb_seq
    outs = _inproj(x_tok, mod, lp['norm1_w'], lp['w_in'], lp['qkw'], rope, tm=tm)
    qkv, zr, gates = outs[:3]
    kv = outs[3] if cache is None else None
    seq = lambda a: a.reshape(a.shape[:-3] + (nb_seq, t_seq, a.shape[-1]))
    oa = _attention(seq(qkv), cache, lp['lam'], lp['subln_w'], lam_init, tq=tq)
    kk, lw, bb, kd, bsum = _rwkv_prep(zr, lp['w_lora_up'], lp['a_lora_up'], lp['w0'], lp['a0'],
                                      lp['k_k'], lp['k_a'], lp['r_k'], tm=512)
    yf, yb, s_fin = _rwkv_scan(seq(zr), ZR_RR, ZR_VR, seq(kk), seq(lw), seq(bb), seq(kd), s0_pairs, nbb=nbb)
    tokv = lambda a: a.reshape(nbm, tmod, a.shape[-1])
    x1, h2 = _merge(x_tok, zr, gates, tokv(yf), tokv(yb), bsum, tokv(oa), mod, lp['ln_x_w'], lp['ln_x_b'],
                    lp['w_attn_br'], lp['w_rwkv_br'], lp['w_out'], lp['norm2_w'], tm=512)
    y = _ffn(h2, x1, mod, lp['w_ffn_in'], lp['w_ffn_out'], tm=512, n_chunks=2)
    return y, kv, s_fin


def kernel(x_prompt, x_sample, cache_k, cache_v, state_rwkv, c, c_ctx, ada_w, ada_b, norm1_w, norm2_w, w_in, q_norm_w, k_norm_w, lambda_q1, lambda_k1, lambda_q2, lambda_k2, subln_w, w_lora_up, w0, a_lora_up, a0, k_k, k_a, r_k, ln_x_w, ln_x_b, w_attn_br, w_rwkv_br, w_out, w_ffn_in, w_ffn_out):
    depth = ada_w.shape[0]
    batch, seq_len, d = x_prompt.shape
    dec_batch, dec_seq, _ = x_sample.shape
    past = cache_k.shape[2]
    qk_cols = N_HEADS_A * 2 * HD_A
    assert qk_cols == COL and W_R == COL and N_HEADS_A * VD_A == COL

    rope = _rope_tables(dec_seq)
    y_prompt = x_prompt.reshape(1, batch * seq_len, d)
    y_sample = x_sample
    ks_out, vs_out, ss_out = [], [], []
    cond_rows = 16
    for li in range(depth):
        cond = jnp.concatenate([c_ctx[None, :], c, jnp.zeros((cond_rows - 1 - dec_batch, d), F32)], axis=0)
        m = _modulation(cond, ada_w[li], ada_b[li])
        mod_ctx = m[0:1].reshape(1, 1, 6 * d)
        mod_lat = m[1:1 + dec_batch].reshape(dec_batch, 1, 6 * d)

        wi = w_in[li]
        o_lora = 3 * COL + 4 * W_R
        o_gate = o_lora + 4 * LORA
        w_re = jnp.concatenate([wi[:, :o_lora], wi[:, o_lora:o_gate], jnp.zeros((d, COL - 4 * LORA), F32),
                                wi[:, o_gate:]], axis=1).astype(BF16)
        assert w_re.shape[1] == N_TILES * COL
        lp = {
            'norm1_w': norm1_w[li], 'norm2_w': norm2_w[li], 'w_in': w_re,
            'qkw': jnp.stack([jnp.tile(q_norm_w[li], 2 * N_HEADS_A), jnp.tile(k_norm_w[li], 2 * N_HEADS_A)]),
            'lam': jnp.stack([lambda_q1[li], lambda_k1[li], lambda_q2[li], lambda_k2[li]]),
            'subln_w': subln_w[li], 'w_lora_up': w_lora_up[li], 'w0': w0[li], 'a_lora_up': a_lora_up[li],
            'a0': a0[li], 'k_k': k_k[li], 'k_a': k_a[li], 'r_k': r_k[li], 'ln_x_w': ln_x_w[li],
            'ln_x_b': ln_x_b[li], 'w_attn_br': w_attn_br[li].astype(BF16), 'w_rwkv_br': w_rwkv_br[li].astype(BF16),
            'w_out': w_out[li].astype(BF16), 'w_ffn_in': w_ffn_in[li].astype(BF16),
            'w_ffn_out': w_ffn_out[li].astype(BF16),
        }
        lam_init = 0.8 - 0.6 * math.exp(-0.3 * li)

        s0_ctx = jnp.zeros((batch, 2, N_HEADS_R // 2, LANES, LANES), F32)
        y_prompt, kv_ctx, s_ctx = _layer(y_prompt, batch, mod_ctx, lp, lam_init, None, s0_ctx, None,
                                         tm=512, tq=seq_len, nbb=2)
        ks_out.append(kv_ctx[0, :, 0:COL].reshape(batch, seq_len, N_HEADS_A, 2, HD_A))
        vs_out.append(kv_ctx[0, :, COL:2 * COL].reshape(batch, seq_len, N_HEADS_A, VD_A))
        ss_out.append(_pairs_to_state(s_ctx))

        cache = (cache_k[:, li].reshape(dec_batch, past, COL).astype(BF16),
                 cache_v[:, li].reshape(dec_batch, past, COL).astype(BF16))
        s0_lat = _state_to_pairs(state_rwkv[:, li])
        y_sample, _, _ = _layer(y_sample, dec_batch, mod_lat, lp, lam_init, cache, s0_lat, rope,
                                tm=512, tq=512, nbb=2)
    new_k = jnp.stack(ks_out, axis=1)
    new_v = jnp.stack(vs_out, axis=1)
    new_state = jnp.stack(ss_out, axis=1)
    return (y_prompt.reshape(batch, seq_len, d), y_sample, new_k, new_v, new_state)
```

```python
import functools
import math

import jax
import jax.numpy as jnp
from jax import lax
from jax.experimental import pallas as pl
from jax.experimental.pallas import tpu as pltpu

F32 = jnp.float32
BF16 = jnp.bfloat16

N_HEADS_A = 4
HD_A = 64
VD_A = 2 * HD_A
N_HEADS_R = 8
HD_R = 64
W_R = N_HEADS_R * HD_R
LORA = 64
GRID_W = 64
ROPE_THETA = 10000.0
EPS_RMS = 1e-6
EPS_GN = 64e-5

LANES = 128
VMEM_LIMIT = 48 * 1024 * 1024
INPROJ_VMEM_LIMIT = 58 * 1024 * 1024

CHUNK = 64
SUB = 8
COL = 512

NN = (((1,), (0,)), ((), ()))
NT = (((1,), (1,)), ((), ()))
TN = (((0,), (0,)), ((), ()))


def _dot(a, b, dims=NN):
    return lax.dot_general(a.astype(BF16), b.astype(BF16), dims, preferred_element_type=F32)


def _split2(x):
    hi = x.astype(BF16)
    lo = (x - hi.astype(F32)).astype(BF16)
    return hi, lo


def _split3(x):
    hi = x.astype(BF16)
    r1 = x - hi.astype(F32)
    mid = r1.astype(BF16)
    lo = (r1 - mid.astype(F32)).astype(BF16)
    return hi, mid, lo


def _dot_x3(a, b, dims=NN):
    ah, al = _split2(a)
    bh, bl = _split2(b)
    d = functools.partial(lax.dot_general, dimension_numbers=dims, preferred_element_type=F32)
    return d(ah, bh) + d(ah, bl) + d(al, bh)


def _sigmoid(x):
    return 1.0 / (1.0 + jnp.exp(-x))


def _idiv(x, pow2):
    assert pow2 & (pow2 - 1) == 0
    return x >> int(math.log2(pow2))


def _group_ones(n, group):
    r = _idiv(lax.broadcasted_iota(jnp.int32, (n, n), 0), group)
    c = _idiv(lax.broadcasted_iota(jnp.int32, (n, n), 1), group)
    return jnp.where(r == c, 1.0, 0.0).astype(BF16)


def _group_sum(x, ones_bd):
    ones2 = jnp.concatenate([ones_bd, ones_bd], axis=0)
    outs = []
    for cb in range(x.shape[1] // LANES):
        hi, lo = _split2(x[:, cb * LANES:(cb + 1) * LANES])
        outs.append(jnp.dot(jnp.concatenate([hi, lo], axis=1), ones2, preferred_element_type=F32))
    return outs[0] if len(outs) == 1 else jnp.concatenate(outs, axis=1)


def _cparams(sem):
    return pltpu.CompilerParams(dimension_semantics=sem, vmem_limit_bytes=VMEM_LIMIT)


def _mod_kernel(c_ref, w_ref, b_ref, o_ref):
    c = c_ref[...]
    s = c * _sigmoid(c)
    o_ref[...] = _dot_x3(s, w_ref[...]) + b_ref[...]


def _modulation(cond, ada_w, ada_b):
    rows, d = cond.shape
    n = ada_w.shape[1]
    tn = 1536
    return pl.pallas_call(
        _mod_kernel,
        grid=(n // tn,),
        in_specs=[pl.BlockSpec((rows, d), lambda j: (0, 0)),
                  pl.BlockSpec((d, tn), lambda j: (0, j)),
                  pl.BlockSpec((1, tn), lambda j: (0, j))],
        out_specs=pl.BlockSpec((rows, tn), lambda j: (0, j)),
        out_shape=jax.ShapeDtypeStruct((rows, n), F32),
        compiler_params=_cparams(("parallel",)),
        name="modulation",
    )(cond, ada_w, ada_b.reshape(1, n))


def _rope_kernel(cos_ref, sin_ref, *, tm):
    i = pl.program_id(0)
    shift = int(math.log2(GRID_W))
    t = i * tm + lax.broadcasted_iota(jnp.int32, (tm, LANES), 0)
    lane = lax.broadcasted_iota(jnp.int32, (tm, LANES), 1)
    l64 = lane & (HD_A - 1)
    nf = HD_A // 4
    f = (l64 & (nf - 1)).astype(F32)
    inv = jnp.exp(f * (-math.log(ROPE_THETA) / nf))
    pos = jnp.where(l64 < HD_A // 2, t >> shift, t & (GRID_W - 1)).astype(F32)
    ang = pos * inv
    sn = jnp.sin(ang)
    cos_ref[...] = jnp.cos(ang)
    sin_ref[...] = jnp.where((l64 & (2 * nf - 1)) < nf, -sn, sn)


def _rope_tables(t_len):
    assert GRID_W & (GRID_W - 1) == 0
    tm = 512
    return pl.pallas_call(
        functools.partial(_rope_kernel, tm=tm),
        grid=(t_len // tm,),
        out_specs=[pl.BlockSpec((tm, LANES), lambda i: (i, 0))] * 2,
        out_shape=[jax.ShapeDtypeStruct((t_len, LANES), F32)] * 2,
        compiler_params=_cparams(("parallel",)),
        name="rope_tables",
    )()


TILE_Q, TILE_K, TILE_V, TILE_RR, TILE_KR, TILE_VR, TILE_GR, TILE_LORA, TILE_GATE = 0, 1, 2, 3, 4, 5, 6, 7, 8
N_TILES = 12
ZR_RR, ZR_KR, ZR_VR, ZR_GR, ZR_LORA = 0, 1, 2, 3, 4
N_ZR = 5
N_GATE = N_TILES - TILE_GATE
Q_SCALE = HD_A ** -0.5 * math.log2(math.e)


def _rope_apply(y, cos, sin):
    n = y.shape[1]
    lane = lax.broadcasted_iota(jnp.int32, y.shape, 1)
    quarter = HD_A // 4
    first = (lane & (2 * quarter - 1)) < quarter
    swapped = jnp.where(first, pltpu.roll(y, n - quarter, 1), pltpu.roll(y, quarter, 1))
    reps = n // LANES
    cos_f = jnp.concatenate([cos] * reps, axis=1)
    sin_f = jnp.concatenate([sin] * reps, axis=1)
    return y * cos_f + swapped * sin_f


def _inproj_kernel(*refs, latent):
    if latent:
        x_ref, mod_ref, nw_ref, w_ref, qkw_ref, cos_ref, sin_ref, qkv_ref, zr_ref, gate_ref = refs
    else:
        x_ref, mod_ref, nw_ref, w_ref, qkw_ref, qkv_ref, zr_ref, gate_ref, k32_ref, v32_ref = refs
    d = x_ref.shape[2]
    x = x_ref[0]
    ms = jnp.mean(x * x, axis=-1, keepdims=True)
    y = x * lax.rsqrt(ms + EPS_RMS) * nw_ref[...]
    h = (y * (1.0 + mod_ref[0, :, d:2 * d]) + mod_ref[0, :, 0:d]).astype(BF16)

    def tile(j):
        return jnp.dot(h, w_ref[:, j * COL:(j + 1) * COL], preferred_element_type=F32)

    def qk_norm(zz, w_row):
        gms = _group_sum(zz * zz, _group_ones(LANES, HD_A)) * (1.0 / HD_A)
        return zz * lax.rsqrt(gms + EPS_RMS) * w_row

    def maybe_rope(v):
        return _rope_apply(v, cos_ref[...], sin_ref[...]) if latent else v

    def col(j):
        return slice(j * COL, (j + 1) * COL)

    order = [TILE_Q, TILE_K, TILE_V] + list(range(TILE_GATE, N_TILES)) + list(range(TILE_RR, TILE_GATE))
    z_next = tile(order[0])
    for pos, j in enumerate(order):
        z = z_next
        if pos + 1 < N_TILES:
            z_next = tile(order[pos + 1])
        if j == TILE_Q:
            qkv_ref[0, :, col(0)] = (maybe_rope(qk_norm(z, qkw_ref[0:1, :])) * Q_SCALE).astype(BF16)
        elif j == TILE_K:
            kn = qk_norm(z, qkw_ref[1:2, :])
            qkv_ref[0, :, col(1)] = maybe_rope(kn).astype(BF16)
            if not latent:
                k32_ref[0] = kn
        elif j == TILE_V:
            qkv_ref[0, :, col(2)] = z.astype(BF16)
            if not latent:
                v32_ref[0] = z
        elif j < TILE_GATE:
            zr_ref[0, :, col(j - TILE_RR)] = z
        else:
            gate_ref[0, :, col(j - TILE_GATE)] = _sigmoid(z)


def _inproj(x, mod, norm1_w, w_bf, qkw, rope, *, tm):
    nb, t_len, d = x.shape
    latent = rope is not None
    ncols = w_bf.shape[1]
    in_specs = [pl.BlockSpec((1, tm, d), lambda b, i: (b, i, 0)),
                pl.BlockSpec((1, 1, mod.shape[2]), lambda b, i: (b, 0, 0)),
                pl.BlockSpec((1, d), lambda b, i: (0, 0)),
                pl.BlockSpec((d, ncols), lambda b, i: (0, 0), pipeline_mode=pl.Buffered(1)),
                pl.BlockSpec((2, COL), lambda b, i: (0, 0))]
    args = [x, mod, norm1_w.reshape(1, d), w_bf, qkw]
    if latent:
        in_specs += [pl.BlockSpec((tm, LANES), lambda b, i: (i, 0))] * 2
        args += list(rope)
    widths = [3, N_ZR, N_GATE] + ([] if latent else [1, 1])
    dtypes = [BF16, F32, F32] + ([] if latent else [F32, F32])
    out_specs = [pl.BlockSpec((1, tm, n * COL), lambda b, i: (b, i, 0)) for n in widths]
    out_shape = [jax.ShapeDtypeStruct((nb, t_len, n * COL), dt) for n, dt in zip(widths, dtypes)]
    return pl.pallas_call(
        functools.partial(_inproj_kernel, latent=latent),
        grid=(nb, t_len // tm),
        in_specs=in_specs,
        out_specs=out_specs,
        out_shape=out_shape,
        compiler_params=pltpu.CompilerParams(dimension_semantics=("parallel", "parallel"),
                                             vmem_limit_bytes=INPROJ_VMEM_LIMIT),
        name="inproj_latent" if latent else "inproj_context",
    )(*args)


KV_CHUNK = 512


def _attn_kernel(*refs, has_cache, lam_init):
    if has_cache:
        q_ref, k_ref, v_ref, ck_ref, cv_ref, lam_ref, sw_ref, o_ref = refs
    else:
        q_ref, k_ref, v_ref, lam_ref, sw_ref, o_ref = refs
    tq = q_ref.shape[1]
    t_k = k_ref.shape[1]
    q = q_ref[0].astype(F32)
    lane = lax.broadcasted_iota(jnp.int32, q.shape, 1)
    qs = jnp.concatenate([jnp.where(lane < HD_A, q, 0.0), jnp.where(lane >= HD_A, q, 0.0)], axis=0).astype(BF16)

    chunks = []
    if has_cache:
        chunks.append((ck_ref[0], cv_ref[0]))
    kc = min(KV_CHUNK, t_k)
    for c in range(t_k // kc):
        chunks.append((k_ref[0, c * kc:(c + 1) * kc, :], v_ref[0, c * kc:(c + 1) * kc, :]))

    def scores(c):
        return lax.dot_general(qs, chunks[c][0], NT, preferred_element_type=F32)

    m = l = acc = None
    s_next = scores(0)
    for c, (_, v_c) in enumerate(chunks):
        s = s_next
        if c + 1 < len(chunks):
            s_next = scores(c + 1)
        mx = jnp.max(s, axis=-1, keepdims=True)
        m_new = mx if m is None else jnp.maximum(m, mx)
        p = jnp.exp2(s - m_new)
        psum = p[:, 0:LANES]
        for cb in range(1, p.shape[1] // LANES):
            psum = psum + p[:, cb * LANES:(cb + 1) * LANES]
        pv = jnp.dot(p.astype(BF16), v_c, preferred_element_type=F32)
        if m is None:
            l, acc = psum, pv
        else:
            alpha = jnp.exp2(m - m_new)
            l = alpha * l + psum
            acc = alpha * acc + pv
        m = m_new
    o = acc / jnp.sum(l, axis=-1, keepdims=True)
    lp = lam_ref[...]
    lam = (jnp.exp(jnp.sum(lp[0:1] * lp[1:2], axis=-1, keepdims=True))
           - jnp.exp(jnp.sum(lp[2:3] * lp[3:4], axis=-1, keepdims=True)) + lam_init)
    o = o[0:tq] - lam * o[tq:2 * tq]
    ms = jnp.mean(o * o, axis=-1, keepdims=True)
    o_ref[0] = (o * lax.rsqrt(ms + EPS_RMS) * sw_ref[...] * (1.0 - lam_init)).astype(o_ref.dtype)


def _attention(qkv, cache, lam_p, subln_w, lam_init, *, tq):
    nb, t_len, _ = qkv.shape
    cpb = COL // LANES
    in_specs = [pl.BlockSpec((1, tq, LANES), lambda b, h, i: (b, i, h)),
                pl.BlockSpec((1, t_len, LANES), lambda b, h, i: (b, 0, cpb + h)),
                pl.BlockSpec((1, t_len, LANES), lambda b, h, i: (b, 0, 2 * cpb + h))]
    args = [qkv, qkv, qkv]
    if cache is not None:
        ck, cv = cache
        past = ck.shape[1]
        in_specs += [pl.BlockSpec((1, past, LANES), lambda b, h, i: (b, 0, h))] * 2
        args += [ck, cv]
    in_specs += [pl.BlockSpec((4, HD_A), lambda b, h, i: (0, 0)),
                 pl.BlockSpec((1, VD_A), lambda b, h, i: (0, 0))]
    args += [lam_p, subln_w.reshape(1, VD_A)]
    return pl.pallas_call(
        functools.partial(_attn_kernel, has_cache=cache is not None, lam_init=lam_init),
        grid=(nb, N_HEADS_A, t_len // tq),
        in_specs=in_specs,
        out_specs=pl.BlockSpec((1, tq, LANES), lambda b, h, i: (b, i, h)),
        out_shape=jax.ShapeDtypeStruct((nb, t_len, N_HEADS_A * VD_A), BF16),
        compiler_params=_cparams(("parallel", "parallel", "arbitrary")),
        name="attention_latent" if cache is not None else "attention_context",
    )(*args)


def _prep_kernel(r_ref, k_ref, la_ref, wup_ref, aup_ref, w0_ref, a0_ref, kk_ref_p, ka_ref, rk_ref,
                 kk_out, lw_out, b_out, kd_out, bs_out):
    r = r_ref[0]
    k = k_ref[0]
    la = la_ref[0]
    ones_bd = _group_ones(LANES, HD_R)
    kraw = k * kk_ref_p[...]
    nrm = jnp.sqrt(_group_sum(kraw * kraw, ones_bd))
    kk = kraw / jnp.maximum(nrm, 1e-12)
    kk_out[0] = kk
    wl = jnp.tanh(la[:, 0:2 * LORA])
    al = la[:, 2 * LORA:4 * LORA]
    lane = lax.broadcasted_iota(jnp.int32, wl.shape, 1)
    rkd = None
    for z in range(2):
        sel = _idiv(lane, LORA) == z
        w = w0_ref[z:z + 1, :] + _dot_x3(jnp.where(sel, wl, 0.0), wup_ref[...])
        lw_out[z, 0] = -math.exp(-0.5) * _sigmoid(w)
        a = _sigmoid(a0_ref[z:z + 1, :] + _dot_x3(jnp.where(sel, al, 0.0), aup_ref[...]))
        kd = k * (1.0 + (a - 1.0) * ka_ref[...])
        b_out[z, 0] = kk * a
        kd_out[z, 0] = kd
        rkd = r * kd if rkd is None else rkd + r * kd
    bs_out[0] = _group_sum(rkd * rk_ref[...], ones_bd)


def _rwkv_prep(z, w_lora_up, a_lora_up, w0, a0, k_k, k_a, r_k, *, tm):
    nb, t_len, _ = z.shape
    tok = lambda col: pl.BlockSpec((1, tm, COL), lambda b, i: (b, i, col))
    par = lambda rows: pl.BlockSpec((rows, W_R), lambda b, i: (0, 0))
    dir_spec = pl.BlockSpec((2, 1, tm, W_R), lambda b, i: (0, b, i, 0))
    tok_out = pl.BlockSpec((1, tm, W_R), lambda b, i: (b, i, 0))
    one = jax.ShapeDtypeStruct((nb, t_len, W_R), F32)
    two = jax.ShapeDtypeStruct((2, nb, t_len, W_R), F32)
    return pl.pallas_call(
        _prep_kernel,
        grid=(nb, t_len // tm),
        in_specs=[tok(ZR_RR), tok(ZR_KR),
                  pl.BlockSpec((1, tm, 4 * LORA), lambda b, i: (b, i, ZR_LORA * COL // (4 * LORA))),
                  par(2 * LORA), par(2 * LORA), par(2), par(2), par(1), par(1), par(1)],
        out_specs=[tok_out, dir_spec, dir_spec, dir_spec, tok_out],
        out_shape=[one, two, two, two, one],
        compiler_params=_cparams(("parallel", "parallel")),
        name="rwkv_prep",
    )(z, z, z, w_lora_up.reshape(2 * LORA, W_R), a_lora_up.reshape(2 * LORA, W_R), w0, a0,
      k_k.reshape(1, W_R), k_a.reshape(1, W_R), r_k.reshape(1, W_R))


def _pair_rows(y):
    lane = lax.broadcasted_iota(jnp.int32, y.shape, 1) & (LANES - 1)
    return jnp.concatenate([jnp.where(lane < HD_R, y, 0.0).astype(BF16),
                            jnp.where(lane >= HD_R, y, 0.0).astype(BF16)], axis=0)


def _rwkv_chunk_kernel(rf, kkf, vf, lwf, bf_, kdf, rb, kkb, vb, lwb, bb_, kdb, s0_ref,
                       yf_ref, yb_ref, sfin_ref, s_scr, *, nbb):
    c = pl.program_id(1)
    n_pairs = W_R // LANES

    @pl.when(c == 0)
    def _():
        s_scr[...] = s0_ref[...]

    C = CHUNK
    t_i = lax.broadcasted_iota(jnp.int32, (C, C), 0)
    i_i = lax.broadcasted_iota(jnp.int32, (C, C), 1)
    t2 = lax.broadcasted_iota(jnp.int32, (C, LANES), 0)
    i2 = lax.broadcasted_iota(jnp.int32, (C, LANES), 1) & (HD_R - 1)
    same_sub = _idiv(t2, SUB) == _idiv(i2, SUB)
    eye = lax.broadcasted_iota(jnp.int32, (LANES, LANES), 0) == lax.broadcasted_iota(jnp.int32, (LANES, LANES), 1)
    same_head = (_idiv(lax.broadcasted_iota(jnp.int32, (LANES, LANES), 0), HD_R)
                 == _idiv(lax.broadcasted_iota(jnp.int32, (LANES, LANES), 1), HD_R))
    masks = {}
    for rev in (False, True):
        tri = (i_i >= t_i) if rev else (i_i <= t_i)
        masks[rev] = dict(tri=jnp.where(tri, 1.0, 0.0).astype(BF16),
                          strict=(i2 > t2) if rev else (i2 < t2),
                          incl=(i2 >= t2) if rev else (i2 <= t2))

    chains = []
    for bi in range(nbb):
        for rev, (r_, kk_, v_, lw_, b_, kd_) in ((False, (rf, kkf, vf, lwf, bf_, kdf)),
                                                  (True, (rb, kkb, vb, lwb, bb_, kdb))):
            for p in range(n_pairs):
                sl = slice(p * LANES, (p + 1) * LANES)
                chains.append(dict(bi=bi, rev=rev, p=p, r=r_[bi, :, sl], kk=kk_[bi, :, sl], v=v_[bi, :, sl],
                                   lw=lw_[bi, :, sl], b=b_[bi, :, sl], kd=kd_[bi, :, sl]))

    for ch in chains:
        h, m, l = _split3(ch['lw'])
        cl = jnp.dot(masks[ch['rev']]['tri'], jnp.concatenate([h, m, l], axis=1), preferred_element_type=F32)
        ch['cl'] = cl[:, 0:LANES] + cl[:, LANES:2 * LANES] + cl[:, 2 * LANES:3 * LANES]
    for ch in chains:
        cl, lw = ch['cl'], ch['lw']
        p_in = jnp.exp(cl)
        p_inv = jnp.exp(-cl)
        p_ex = jnp.exp(cl - lw)
        ch['pc'] = jnp.exp(jnp.sum(lw, axis=0, keepdims=True))
        p_end = ch['pc'] * p_inv
        ch['at'] = -ch['kk'] * p_ex
        ch['rt'] = ch['r'] * p_in
        ch['bt'] = ch['b'] * p_inv
        ch['kt'] = ch['kd'] * p_inv
        ch['bh'] = ch['b'] * p_end
        ch['kh'] = ch['kd'] * p_end
        ch['S'] = s_scr[ch['bi'], int(ch['rev']), ch['p']]
    for ch in chains:
        g = lax.dot_general(jnp.concatenate([ch['at'], ch['rt']], axis=0).astype(BF16),
                            jnp.concatenate([_pair_rows(ch['bt']), _pair_rows(ch['kt'])], axis=0),
                            NT, preferred_element_type=F32)
        mk = masks[ch['rev']]
        ch['lab'] = jnp.where(mk['strict'], g[0:C, 0:LANES], 0.0)
        ch['lak'] = jnp.where(mk['strict'], g[0:C, LANES:2 * LANES], 0.0)
        ch['lrb'] = jnp.where(mk['incl'], g[C:2 * C, 0:LANES], 0.0)
        ch['lrk'] = jnp.where(mk['incl'], g[C:2 * C, LANES:2 * LANES], 0.0)
        ch['vbd'] = _pair_rows(ch['v'])
    for ch in chains:
        x0 = jnp.dot(jnp.concatenate([ch['at'], ch['lak']], axis=1).astype(BF16),
                     jnp.concatenate([ch['S'].astype(BF16), ch['vbd']], axis=0), preferred_element_type=F32)
        nd = jnp.where(same_sub, ch['lab'], 0.0)
        no = jnp.where(same_sub, 0.0, ch['lab'])
        ch['zc'] = jnp.concatenate([x0, no], axis=1)
        ch['nj'] = nd
    n1 = int(math.log2(SUB))
    for lev in range(n1):
        for ch in chains:
            if lev < n1 - 1:
                rr = jnp.dot(ch['nj'].astype(BF16), _pair_rows(jnp.concatenate([ch['zc'], ch['nj']], axis=1)),
                             preferred_element_type=F32)
                ch['zc'] = ch['zc'] + rr[:, 0:2 * LANES]
                ch['nj'] = rr[:, 2 * LANES:3 * LANES]
            else:
                ch['zc'] = ch['zc'] + jnp.dot(ch['nj'].astype(BF16), _pair_rows(ch['zc']),
                                              preferred_element_type=F32)
    for ch in chains:
        ch['x'] = ch['zc'][:, 0:LANES]
        ch['mj'] = ch['zc'][:, LANES:2 * LANES]
    n2 = int(math.log2(C // SUB))
    for lev in range(n2):
        for ch in chains:
            if lev < n2 - 1:
                rr = jnp.dot(ch['mj'].astype(BF16), _pair_rows(jnp.concatenate([ch['x'], ch['mj']], axis=1)),
                             preferred_element_type=F32)
                ch['x'] = ch['x'] + rr[:, 0:LANES]
                ch['mj'] = rr[:, LANES:2 * LANES]
            else:
                ch['x'] = ch['x'] + jnp.dot(ch['mj'].astype(BF16), _pair_rows(ch['x']), preferred_element_type=F32)
    for ch in chains:
        u = ch['x']
        ubd = _pair_rows(u)
        y = jnp.dot(jnp.concatenate([ch['rt'], ch['lrb'], ch['lrk']], axis=1).astype(BF16),
                    jnp.concatenate([ch['S'].astype(BF16), ubd, ch['vbd']], axis=0), preferred_element_type=F32)
        dg = jnp.where(eye, jnp.broadcast_to(ch['pc'], (LANES, LANES)), 0.0)
        s_new = lax.dot_general(jnp.concatenate([ch['bh'], ch['kh'], dg], axis=0).astype(BF16),
                                jnp.concatenate([u, ch['v'], ch['S']], axis=0).astype(BF16),
                                TN, preferred_element_type=F32)
        s_new = jnp.where(same_head, s_new, 0.0)
        sl = slice(ch['p'] * LANES, (ch['p'] + 1) * LANES)
        if ch['rev']:
            yb_ref[ch['bi'], :, sl] = y
        else:
            yf_ref[ch['bi'], :, sl] = y
        s_scr[ch['bi'], int(ch['rev']), ch['p']] = s_new

    @pl.when(c == pl.num_programs(1) - 1)
    def _():
        sfin_ref[...] = s_scr[...]


def _rwkv_scan(r_src, r_col, v_col, kk, lw, bb, kd, s0, *, nbb):
    nb, t_len, _ = kk.shape
    nc = t_len // CHUNK
    n_pairs = W_R // LANES

    def tok(col, rev):
        return pl.BlockSpec((nbb, CHUNK, W_R), (lambda g, c: (g, nc - 1 - c, col)) if rev else (lambda g, c: (g, c, col)))

    def dirs(z, rev):
        return pl.BlockSpec((None, nbb, CHUNK, W_R),
                            (lambda g, c: (z, g, nc - 1 - c, 0)) if rev else (lambda g, c: (z, g, c, 0)))

    st_spec = pl.BlockSpec((nbb, 2, n_pairs, LANES, LANES), lambda g, c: (g, 0, 0, 0, 0))
    in_specs, args = [], []
    for rev in (False, True):
        z = int(rev)
        in_specs += [tok(r_col, rev), tok(0, rev), tok(v_col, rev), dirs(z, rev), dirs(z, rev), dirs(z, rev)]
        args += [r_src, kk, r_src, lw, bb, kd]
    in_specs.append(st_spec)
    args.append(s0)
    yshape = jax.ShapeDtypeStruct((nb, t_len, W_R), F32)
    return pl.pallas_call(
        functools.partial(_rwkv_chunk_kernel, nbb=nbb),
        grid=(nb // nbb, nc),
        in_specs=in_specs,
        out_specs=[tok(0, False), tok(0, True), st_spec],
        out_shape=[yshape, yshape, jax.ShapeDtypeStruct(s0.shape, F32)],
        scratch_shapes=[pltpu.VMEM((nbb, 2, n_pairs, LANES, LANES), F32)],
        compiler_params=_cparams(("parallel", "arbitrary")),
        name="rwkv_scan",
    )(*args)


def _post_kernel(x_ref, yf_ref, yb_ref, bs_ref, vr_ref, gr_ref, oa_ref, ga_ref, gg_ref, mod_ref,
                 lnw_ref, lnb_ref, wa_ref, wr_ref, wo_ref, n2_ref, wi_ref, wo2_ref, o_ref, *, n_chunks):
    d = x_ref.shape[2]
    ones_bd = _group_ones(LANES, HD_R)
    y = yf_ref[0] + yb_ref[0]
    mu = _group_sum(y, ones_bd) * (1.0 / HD_R)
    yc = y - mu
    var = _group_sum(yc * yc, ones_bd) * (1.0 / HD_R)
    yn = yc * lax.rsqrt(var + EPS_GN) * lnw_ref[...] + lnb_ref[...]
    o_r = (yn + bs_ref[0] * vr_ref[0]) * _sigmoid(gr_ref[0])
    merged = ga_ref[0] * _dot(oa_ref[0], wa_ref[...]) + gg_ref[0] * _dot(o_r, wr_ref[...])
    g1 = mod_ref[0, :, 2 * d:3 * d]
    x1 = x_ref[0] + g1 * _dot(merged, wo_ref[...])
    sh2 = mod_ref[0, :, 3 * d:4 * d]
    sc2 = mod_ref[0, :, 4 * d:5 * d]
    ms = jnp.mean(x1 * x1, axis=-1, keepdims=True)
    h = (x1 * lax.rsqrt(ms + EPS_RMS) * n2_ref[...] * (1.0 + sc2) + sh2).astype(BF16)

    d_ff = wo2_ref.shape[0]
    tf = d_ff // n_chunks

    def up(c):
        u = jnp.dot(h, wi_ref[:, c * tf:(c + 1) * tf], preferred_element_type=F32)
        g = jnp.dot(h, wi_ref[:, d_ff + c * tf:d_ff + (c + 1) * tf], preferred_element_type=F32)
        return u, g

    acc = None
    nxt = up(0)
    for c in range(n_chunks):
        u, g = nxt
        if c + 1 < n_chunks:
            nxt = up(c + 1)
        a = ((u * _sigmoid(u)) * g).astype(BF16)
        part = jnp.dot(a, wo2_ref[c * tf:(c + 1) * tf, :], preferred_element_type=F32)
        acc = part if acc is None else acc + part
    o_ref[0] = x1 + mod_ref[0, :, 5 * d:6 * d] * acc


def _post(x, zr, gates, yf, yb, bsum, oa, mod, ln_x_w, ln_x_b, wa_bf, wr_bf, wo_bf, norm2_w, w_in_bf, w_out_bf,
          *, tm, n_chunks):
    nb, t_len, d = x.shape
    tok = lambda w, col: pl.BlockSpec((1, tm, w), lambda b, i: (b, i, col))
    resident = lambda a: pl.BlockSpec(a.shape, lambda b, i: (0,) * a.ndim, pipeline_mode=pl.Buffered(1))
    lnw, lnb, n2 = ln_x_w.reshape(1, W_R), ln_x_b.reshape(1, W_R), norm2_w.reshape(1, d)
    return pl.pallas_call(
        functools.partial(_post_kernel, n_chunks=n_chunks),
        grid=(nb, t_len // tm),
        in_specs=[tok(d, 0), tok(W_R, 0), tok(W_R, 0), tok(W_R, 0), tok(COL, ZR_VR), tok(COL, ZR_GR),
                  tok(W_R, 0), tok(d, 0), tok(d, 1),
                  pl.BlockSpec((1, 1, mod.shape[2]), lambda b, i: (b, 0, 0)),
                  resident(lnw), resident(lnb), resident(wa_bf), resident(wr_bf), resident(wo_bf), resident(n2),
                  resident(w_in_bf), resident(w_out_bf)],
        out_specs=tok(d, 0),
        out_shape=jax.ShapeDtypeStruct((nb, t_len, d), F32),
        compiler_params=pltpu.CompilerParams(dimension_semantics=("parallel", "parallel"),
                                             vmem_limit_bytes=INPROJ_VMEM_LIMIT),
        name="post",
    )(x, yf, yb, bsum, zr, zr, oa, gates, gates, mod, lnw, lnb, wa_bf, wr_bf, wo_bf, n2, w_in_bf, w_out_bf)


def _state_to_pairs(s):
    nb = s.shape[0]
    st = jnp.swapaxes(s, -1, -2).reshape(nb, 2, N_HEADS_R // 2, 2, HD_R, HD_R)
    zero = jnp.zeros_like(st[:, :, :, 0])
    top = jnp.concatenate([st[:, :, :, 0], zero], axis=-1)
    bot = jnp.concatenate([zero, st[:, :, :, 1]], axis=-1)
    return jnp.concatenate([top, bot], axis=-2)


def _pairs_to_state(sp):
    nb = sp.shape[0]
    h0 = sp[:, :, :, 0:HD_R, 0:HD_R]
    h1 = sp[:, :, :, HD_R:, HD_R:]
    st = jnp.stack([h0, h1], axis=3).reshape(nb, 2, N_HEADS_R, HD_R, HD_R)
    return jnp.swapaxes(st, -1, -2)


def _layer(x_tok, nb_seq, mod, lp, lam_init, cache, s0_pairs, rope, *, tm, tq, nbb):
    nbm, tmod, d = x_tok.shape
    t_seq = nbm * tmod // nb_seq
    outs = _inproj(x_tok, mod, lp['norm1_w'], lp['w_in'], lp['qkw'], rope, tm=tm)
    qkv, zr, gates = outs[:3]
    kv = outs[3:5] if cache is None else None
    seq = lambda a: a.reshape(a.shape[:-3] + (nb_seq, t_seq, a.shape[-1]))
    oa = _attention(seq(qkv), cache, lp['lam'], lp['subln_w'], lam_init, tq=tq)
    kk, lw, bb, kd, bsum = _rwkv_prep(zr, lp['w_lora_up'], lp['a_lora_up'], lp['w0'], lp['a0'],
                                      lp['k_k'], lp['k_a'], lp['r_k'], tm=512)
    yf, yb, s_fin = _rwkv_scan(seq(zr), ZR_RR, ZR_VR, seq(kk), seq(lw), seq(bb), seq(kd), s0_pairs, nbb=nbb)
    tokv = lambda a: a.reshape(nbm, tmod, a.shape[-1])
    y = _post(x_tok, zr, gates, tokv(yf), tokv(yb), bsum, tokv(oa), mod, lp['ln_x_w'], lp['ln_x_b'],
              lp['w_attn_br'], lp['w_rwkv_br'], lp['w_out'], lp['norm2_w'], lp['w_ffn_in'], lp['w_ffn_out'],
              tm=256, n_chunks=2)
    return y, kv, s_fin


def kernel(x_prompt, x_sample, cache_k, cache_v, state_rwkv, c, c_ctx, ada_w, ada_b, norm1_w, norm2_w, w_in, q_norm_w, k_norm_w, lambda_q1, lambda_k1, lambda_q2, lambda_k2, subln_w, w_lora_up, w0, a_lora_up, a0, k_k, k_a, r_k, ln_x_w, ln_x_b, w_attn_br, w_rwkv_br, w_out, w_ffn_in, w_ffn_out):
    depth = ada_w.shape[0]
    batch, seq_len, d = x_prompt.shape
    dec_batch, dec_seq, _ = x_sample.shape
    past = cache_k.shape[2]
    qk_cols = N_HEADS_A * 2 * HD_A
    assert qk_cols == COL and W_R == COL and N_HEADS_A * VD_A == COL

    rope = _rope_tables(dec_seq)
    y_prompt = x_prompt.reshape(1, batch * seq_len, d)
    y_sample = x_sample
    ks_out, vs_out, ss_out = [], [], []
    cond_rows = 16
    for li in range(depth):
        cond = jnp.concatenate([c_ctx[None, :], c, jnp.zeros((cond_rows - 1 - dec_batch, d), F32)], axis=0)
        m = _modulation(cond, ada_w[li], ada_b[li])
        mod_ctx = m[0:1].reshape(1, 1, 6 * d)
        mod_lat = m[1:1 + dec_batch].reshape(dec_batch, 1, 6 * d)

        wi = w_in[li]
        o_lora = 3 * COL + 4 * W_R
        o_gate = o_lora + 4 * LORA
        w_re = jnp.concatenate([wi[:, :o_lora], wi[:, o_lora:o_gate], jnp.zeros((d, COL - 4 * LORA), F32),
                                wi[:, o_gate:]], axis=1).astype(BF16)
        assert w_re.shape[1] == N_TILES * COL
        lp = {
            'norm1_w': norm1_w[li], 'norm2_w': norm2_w[li], 'w_in': w_re,
            'qkw': jnp.stack([jnp.tile(q_norm_w[li], 2 * N_HEADS_A), jnp.tile(k_norm_w[li], 2 * N_HEADS_A)]),
            'lam': jnp.stack([lambda_q1[li], lambda_k1[li], lambda_q2[li], lambda_k2[li]]),
            'subln_w': subln_w[li], 'w_lora_up': w_lora_up[li], 'w0': w0[li], 'a_lora_up': a_lora_up[li],
            'a0': a0[li], 'k_k': k_k[li], 'k_a': k_a[li], 'r_k': r_k[li], 'ln_x_w': ln_x_w[li],
            'ln_x_b': ln_x_b[li], 'w_attn_br': w_attn_br[li].astype(BF16), 'w_rwkv_br': w_rwkv_br[li].astype(BF16),
            'w_out': w_out[li].astype(BF16), 'w_ffn_in': w_ffn_in[li].astype(BF16),
            'w_ffn_out': w_ffn_out[li].astype(BF16),
        }
        lam_init = 0.8 - 0.6 * math.exp(-0.3 * li)

        s0_ctx = jnp.zeros((batch, 2, N_HEADS_R // 2, LANES, LANES), F32)
        y_prompt, kv_ctx, s_ctx = _layer(y_prompt, batch, mod_ctx, lp, lam_init, None, s0_ctx, None,
                                         tm=512, tq=seq_len, nbb=2)
        ks_out.append(kv_ctx[0].reshape(batch, seq_len, N_HEADS_A, 2, HD_A))
        vs_out.append(kv_ctx[1].reshape(batch, seq_len, N_HEADS_A, VD_A))
        ss_out.append(_pairs_to_state(s_ctx))

        cache = (cache_k[:, li].reshape(dec_batch, past, COL).astype(BF16),
                 cache_v[:, li].reshape(dec_batch, past, COL).astype(BF16))
        s0_lat = _state_to_pairs(state_rwkv[:, li])
        y_sample, _, _ = _layer(y_sample, dec_batch, mod_lat, lp, lam_init, cache, s0_lat, rope,
                                tm=512, tq=512, nbb=2)
    new_k = jnp.stack(ks_out, axis=1)
    new_v = jnp.stack(vs_out, axis=1)
    new_state = jnp.stack(ss_out, axis=1)
    return (y_prompt.reshape(batch, seq_len, d), y_sample, new_k, new_v, new_state)
```

```python
import functools
import math

import jax
import jax.numpy as jnp
from jax import lax
from jax.experimental import pallas as pl
from jax.experimental.pallas import tpu as pltpu

F32 = jnp.float32
BF16 = jnp.bfloat16

N_HEADS_A = 4
HD_A = 64
VD_A = 2 * HD_A
N_HEADS_R = 8
HD_R = 64
W_R = N_HEADS_R * HD_R
LORA = 64
GRID_W = 64
ROPE_THETA = 10000.0
EPS_RMS = 1e-6
EPS_GN = 64e-5

LANES = 128
VMEM_LIMIT = 48 * 1024 * 1024
INPROJ_VMEM_LIMIT = 58 * 1024 * 1024

CHUNK = 64
SUB = 8
COL = 512

NN = (((1,), (0,)), ((), ()))
NT = (((1,), (1,)), ((), ()))
TN = (((0,), (0,)), ((), ()))


def _dot(a, b, dims=NN):
    return lax.dot_general(a.astype(BF16), b.astype(BF16), dims, preferred_element_type=F32)


def _split2(x):
    hi = x.astype(BF16)
    lo = (x - hi.astype(F32)).astype(BF16)
    return hi, lo


def _split3(x):
    hi = x.astype(BF16)
    r1 = x - hi.astype(F32)
    mid = r1.astype(BF16)
    lo = (r1 - mid.astype(F32)).astype(BF16)
    return hi, mid, lo


def _dot_x3(a, b, dims=NN):
    ah, al = _split2(a)
    bh, bl = _split2(b)
    d = functools.partial(lax.dot_general, dimension_numbers=dims, preferred_element_type=F32)
    return d(ah, bh) + d(ah, bl) + d(al, bh)


def _sigmoid(x):
    return 1.0 / (1.0 + jnp.exp(-x))


def _idiv(x, pow2):
    assert pow2 & (pow2 - 1) == 0
    return x >> int(math.log2(pow2))


def _group_ones(n, group):
    r = _idiv(lax.broadcasted_iota(jnp.int32, (n, n), 0), group)
    c = _idiv(lax.broadcasted_iota(jnp.int32, (n, n), 1), group)
    return jnp.where(r == c, 1.0, 0.0).astype(BF16)


def _group_sum(x, ones_bd):
    ones2 = jnp.concatenate([ones_bd, ones_bd], axis=0)
    outs = []
    for cb in range(x.shape[1] // LANES):
        hi, lo = _split2(x[:, cb * LANES:(cb + 1) * LANES])
        outs.append(jnp.dot(jnp.concatenate([hi, lo], axis=1), ones2, preferred_element_type=F32))
    return outs[0] if len(outs) == 1 else jnp.concatenate(outs, axis=1)


def _cparams(sem):
    return pltpu.CompilerParams(dimension_semantics=sem, vmem_limit_bytes=VMEM_LIMIT)


def _mod_kernel(c_ref, w_ref, b_ref, o_ref):
    c = c_ref[...]
    s = c * _sigmoid(c)
    o_ref[...] = _dot_x3(s, w_ref[...]) + b_ref[...]


def _modulation(cond, ada_w, ada_b):
    rows, d = cond.shape
    n = ada_w.shape[1]
    tn = 1536
    return pl.pallas_call(
        _mod_kernel,
        grid=(n // tn,),
        in_specs=[pl.BlockSpec((rows, d), lambda j: (0, 0)),
                  pl.BlockSpec((d, tn), lambda j: (0, j)),
                  pl.BlockSpec((1, tn), lambda j: (0, j))],
        out_specs=pl.BlockSpec((rows, tn), lambda j: (0, j)),
        out_shape=jax.ShapeDtypeStruct((rows, n), F32),
        compiler_params=_cparams(("parallel",)),
        name="modulation",
    )(cond, ada_w, ada_b.reshape(1, n))


def _rope_kernel(cos_ref, sin_ref, *, tm):
    i = pl.program_id(0)
    shift = int(math.log2(GRID_W))
    t = i * tm + lax.broadcasted_iota(jnp.int32, (tm, LANES), 0)
    lane = lax.broadcasted_iota(jnp.int32, (tm, LANES), 1)
    l64 = lane & (HD_A - 1)
    nf = HD_A // 4
    f = (l64 & (nf - 1)).astype(F32)
    inv = jnp.exp(f * (-math.log(ROPE_THETA) / nf))
    pos = jnp.where(l64 < HD_A // 2, t >> shift, t & (GRID_W - 1)).astype(F32)
    ang = pos * inv
    sn = jnp.sin(ang)
    cos_ref[...] = jnp.cos(ang)
    sin_ref[...] = jnp.where((l64 & (2 * nf - 1)) < nf, -sn, sn)


def _rope_tables(t_len):
    assert GRID_W & (GRID_W - 1) == 0
    tm = 512
    return pl.pallas_call(
        functools.partial(_rope_kernel, tm=tm),
        grid=(t_len // tm,),
        out_specs=[pl.BlockSpec((tm, LANES), lambda i: (i, 0))] * 2,
        out_shape=[jax.ShapeDtypeStruct((t_len, LANES), F32)] * 2,
        compiler_params=_cparams(("parallel",)),
        name="rope_tables",
    )()


TILE_Q, TILE_K, TILE_V, TILE_RR, TILE_KR, TILE_VR, TILE_GR, TILE_LORA, TILE_GATE = 0, 1, 2, 3, 4, 5, 6, 7, 8
N_TILES = 12
ZR_RR, ZR_KR, ZR_VR, ZR_GR, ZR_LORA = 0, 1, 2, 3, 4
N_ZR = 5
N_GATE = N_TILES - TILE_GATE
Q_SCALE = HD_A ** -0.5 * math.log2(math.e)


def _rope_apply(y, cos, sin):
    n = y.shape[1]
    lane = lax.broadcasted_iota(jnp.int32, y.shape, 1)
    quarter = HD_A // 4
    first = (lane & (2 * quarter - 1)) < quarter
    swapped = jnp.where(first, pltpu.roll(y, n - quarter, 1), pltpu.roll(y, quarter, 1))
    reps = n // LANES
    cos_f = jnp.concatenate([cos] * reps, axis=1)
    sin_f = jnp.concatenate([sin] * reps, axis=1)
    return y * cos_f + swapped * sin_f


def _inproj_kernel(*refs, latent):
    if latent:
        x_ref, mod_ref, nw_ref, w_ref, qkw_ref, cos_ref, sin_ref, qkv_ref, zr_ref, gate_ref = refs
    else:
        x_ref, mod_ref, nw_ref, w_ref, qkw_ref, qkv_ref, zr_ref, gate_ref, k32_ref, v32_ref = refs
    d = x_ref.shape[2]
    x = x_ref[0]
    ms = jnp.mean(x * x, axis=-1, keepdims=True)
    y = x * lax.rsqrt(ms + EPS_RMS) * nw_ref[...]
    h = (y * (1.0 + mod_ref[0, :, d:2 * d]) + mod_ref[0, :, 0:d]).astype(BF16)

    def tile(j):
        return jnp.dot(h, w_ref[:, j * COL:(j + 1) * COL], preferred_element_type=F32)

    def qk_norm(zz, w_row):
        gms = _group_sum(zz * zz, _group_ones(LANES, HD_A)) * (1.0 / HD_A)
        return zz * lax.rsqrt(gms + EPS_RMS) * w_row

    def maybe_rope(v):
        return _rope_apply(v, cos_ref[...], sin_ref[...]) if latent else v

    def col(j):
        return slice(j * COL, (j + 1) * COL)

    order = [TILE_Q, TILE_K, TILE_V] + list(range(TILE_GATE, N_TILES)) + list(range(TILE_RR, TILE_GATE))
    z_next = tile(order[0])
    for pos, j in enumerate(order):
        z = z_next
        if pos + 1 < N_TILES:
            z_next = tile(order[pos + 1])
        if j == TILE_Q:
            qkv_ref[0, :, col(0)] = (maybe_rope(qk_norm(z, qkw_ref[0:1, :])) * Q_SCALE).astype(BF16)
        elif j == TILE_K:
            kn = qk_norm(z, qkw_ref[1:2, :])
            qkv_ref[0, :, col(1)] = maybe_rope(kn).astype(BF16)
            if not latent:
                k32_ref[0] = kn
        elif j == TILE_V:
            qkv_ref[0, :, col(2)] = z.astype(BF16)
            if not latent:
                v32_ref[0] = z
        elif j < TILE_GATE:
            zr_ref[0, :, col(j - TILE_RR)] = z
        else:
            gate_ref[0, :, col(j - TILE_GATE)] = _sigmoid(z)


def _inproj(x, mod, norm1_w, w_bf, qkw, rope, *, tm):
    nb, t_len, d = x.shape
    latent = rope is not None
    ncols = w_bf.shape[1]
    in_specs = [pl.BlockSpec((1, tm, d), lambda b, i: (b, i, 0)),
                pl.BlockSpec((1, 1, mod.shape[2]), lambda b, i: (b, 0, 0)),
                pl.BlockSpec((1, d), lambda b, i: (0, 0)),
                pl.BlockSpec((d, ncols), lambda b, i: (0, 0), pipeline_mode=pl.Buffered(1)),
                pl.BlockSpec((2, COL), lambda b, i: (0, 0))]
    args = [x, mod, norm1_w.reshape(1, d), w_bf, qkw]
    if latent:
        in_specs += [pl.BlockSpec((tm, LANES), lambda b, i: (i, 0))] * 2
        args += list(rope)
    widths = [3, N_ZR, N_GATE] + ([] if latent else [1, 1])
    dtypes = [BF16, F32, F32] + ([] if latent else [F32, F32])
    out_specs = [pl.BlockSpec((1, tm, n * COL), lambda b, i: (b, i, 0)) for n in widths]
    out_shape = [jax.ShapeDtypeStruct((nb, t_len, n * COL), dt) for n, dt in zip(widths, dtypes)]
    return pl.pallas_call(
        functools.partial(_inproj_kernel, latent=latent),
        grid=(nb, t_len // tm),
        in_specs=in_specs,
        out_specs=out_specs,
        out_shape=out_shape,
        compiler_params=pltpu.CompilerParams(dimension_semantics=("parallel", "parallel"),
                                             vmem_limit_bytes=INPROJ_VMEM_LIMIT),
        name="inproj_latent" if latent else "inproj_context",
    )(*args)


KV_CHUNK = 512


def _attn_head(q, k_ref, v_ref, cache_refs, hs, lam, sw, lam_init):
    tq = q.shape[0]
    t_k = k_ref.shape[1]
    lane = lax.broadcasted_iota(jnp.int32, q.shape, 1)
    qs = jnp.concatenate([jnp.where(lane < HD_A, q, 0.0), jnp.where(lane >= HD_A, q, 0.0)], axis=0).astype(BF16)

    chunks = []
    if cache_refs is not None:
        chunks.append((cache_refs[0][0, :, hs], cache_refs[1][0, :, hs]))
    kc = min(KV_CHUNK, t_k)
    for c in range(t_k // kc):
        chunks.append((k_ref[0, c * kc:(c + 1) * kc, hs], v_ref[0, c * kc:(c + 1) * kc, hs]))

    def scores(c):
        return lax.dot_general(qs, chunks[c][0], NT, preferred_element_type=F32)

    m = acc = None
    s_next = scores(0)
    for c, (_, v_c) in enumerate(chunks):
        s = s_next
        if c + 1 < len(chunks):
            s_next = scores(c + 1)
        mx = jnp.max(s, axis=-1, keepdims=True)
        m_new = mx if m is None else jnp.maximum(m, mx)
        p = jnp.exp2(s - m_new)
        v_aug = jnp.concatenate([v_c, jnp.ones_like(v_c)], axis=1)
        pv = jnp.dot(p.astype(BF16), v_aug, preferred_element_type=F32)
        acc = pv if m is None else jnp.exp2(m - m_new) * acc + pv
        m = m_new
    o = acc[:, 0:VD_A] / acc[:, VD_A:2 * VD_A]
    o = o[0:tq] - lam * o[tq:2 * tq]
    ms = jnp.mean(o * o, axis=-1, keepdims=True)
    return o * lax.rsqrt(ms + EPS_RMS) * sw * (1.0 - lam_init)


def _attn_kernel(*refs, has_cache, lam_init):
    if has_cache:
        q_ref, k_ref, v_ref, ck_ref, cv_ref, lam_ref, sw_ref, o_ref = refs
        cache_refs = (ck_ref, cv_ref)
    else:
        q_ref, k_ref, v_ref, lam_ref, sw_ref, o_ref = refs
        cache_refs = None
    lp = lam_ref[...]
    lam = (jnp.exp(jnp.sum(lp[0:1] * lp[1:2], axis=-1, keepdims=True))
           - jnp.exp(jnp.sum(lp[2:3] * lp[3:4], axis=-1, keepdims=True)) + lam_init)
    for h in range(q_ref.shape[2] // LANES):
        hs = slice(h * LANES, (h + 1) * LANES)
        o = _attn_head(q_ref[0, :, hs].astype(F32), k_ref, v_ref, cache_refs, hs, lam, sw_ref[...], lam_init)
        o_ref[0, :, hs] = o.astype(o_ref.dtype)


def _attention(qkv, cache, lam_p, subln_w, lam_init, *, tq, heads_per_step):
    nb, t_len, _ = qkv.shape
    hw = heads_per_step * LANES
    n_hg = N_HEADS_A // heads_per_step
    in_specs = [pl.BlockSpec((1, tq, hw), lambda b, h, i: (b, i, h)),
                pl.BlockSpec((1, t_len, hw), lambda b, h, i: (b, 0, n_hg + h)),
                pl.BlockSpec((1, t_len, hw), lambda b, h, i: (b, 0, 2 * n_hg + h))]
    args = [qkv, qkv, qkv]
    if cache is not None:
        ck, cv = cache
        past = ck.shape[1]
        in_specs += [pl.BlockSpec((1, past, hw), lambda b, h, i: (b, 0, h))] * 2
        args += [ck, cv]
    in_specs += [pl.BlockSpec((4, HD_A), lambda b, h, i: (0, 0)),
                 pl.BlockSpec((1, VD_A), lambda b, h, i: (0, 0))]
    args += [lam_p, subln_w.reshape(1, VD_A)]
    return pl.pallas_call(
        functools.partial(_attn_kernel, has_cache=cache is not None, lam_init=lam_init),
        grid=(nb, n_hg, t_len // tq),
        in_specs=in_specs,
        out_specs=pl.BlockSpec((1, tq, hw), lambda b, h, i: (b, i, h)),
        out_shape=jax.ShapeDtypeStruct((nb, t_len, N_HEADS_A * VD_A), BF16),
        compiler_params=_cparams(("parallel", "parallel", "arbitrary")),
        name="attention_latent" if cache is not None else "attention_context",
    )(*args)


def _prep_kernel(r_ref, k_ref, la_ref, wup_ref, aup_ref, w0_ref, a0_ref, kk_ref_p, ka_ref, rk_ref,
                 kk_out, lw_out, b_out, kd_out, bs_out):
    r = r_ref[0]
    k = k_ref[0]
    la = la_ref[0]
    ones_bd = _group_ones(LANES, HD_R)
    kraw = k * kk_ref_p[...]
    nrm = jnp.sqrt(_group_sum(kraw * kraw, ones_bd))
    kk = kraw / jnp.maximum(nrm, 1e-12)
    kk_out[0] = kk
    wl = jnp.tanh(la[:, 0:2 * LORA])
    al = la[:, 2 * LORA:4 * LORA]
    lane = lax.broadcasted_iota(jnp.int32, wl.shape, 1)
    rkd = None
    for z in range(2):
        sel = _idiv(lane, LORA) == z
        w = w0_ref[z:z + 1, :] + _dot_x3(jnp.where(sel, wl, 0.0), wup_ref[...])
        lw_out[z, 0] = -math.exp(-0.5) * _sigmoid(w)
        a = _sigmoid(a0_ref[z:z + 1, :] + _dot_x3(jnp.where(sel, al, 0.0), aup_ref[...]))
        kd = k * (1.0 + (a - 1.0) * ka_ref[...])
        b_out[z, 0] = kk * a
        kd_out[z, 0] = kd
        rkd = r * kd if rkd is None else rkd + r * kd
    bs_out[0] = _group_sum(rkd * rk_ref[...], ones_bd)


def _rwkv_prep(z, w_lora_up, a_lora_up, w0, a0, k_k, k_a, r_k, *, tm):
    nb, t_len, _ = z.shape
    tok = lambda col: pl.BlockSpec((1, tm, COL), lambda b, i: (b, i, col))
    par = lambda rows: pl.BlockSpec((rows, W_R), lambda b, i: (0, 0))
    dir_spec = pl.BlockSpec((2, 1, tm, W_R), lambda b, i: (0, b, i, 0))
    tok_out = pl.BlockSpec((1, tm, W_R), lambda b, i: (b, i, 0))
    one = jax.ShapeDtypeStruct((nb, t_len, W_R), F32)
    two = jax.ShapeDtypeStruct((2, nb, t_len, W_R), F32)
    return pl.pallas_call(
        _prep_kernel,
        grid=(nb, t_len // tm),
        in_specs=[tok(ZR_RR), tok(ZR_KR),
                  pl.BlockSpec((1, tm, 4 * LORA), lambda b, i: (b, i, ZR_LORA * COL // (4 * LORA))),
                  par(2 * LORA), par(2 * LORA), par(2), par(2), par(1), par(1), par(1)],
        out_specs=[tok_out, dir_spec, dir_spec, dir_spec, tok_out],
        out_shape=[one, two, two, two, one],
        compiler_params=_cparams(("parallel", "parallel")),
        name="rwkv_prep",
    )(z, z, z, w_lora_up.reshape(2 * LORA, W_R), a_lora_up.reshape(2 * LORA, W_R), w0, a0,
      k_k.reshape(1, W_R), k_a.reshape(1, W_R), r_k.reshape(1, W_R))


def _pair_rows(y):
    lane = lax.broadcasted_iota(jnp.int32, y.shape, 1) & (LANES - 1)
    return jnp.concatenate([jnp.where(lane < HD_R, y, 0.0).astype(BF16),
                            jnp.where(lane >= HD_R, y, 0.0).astype(BF16)], axis=0)


def _rwkv_chunk_kernel(rf, kkf, vf, lwf, bf_, kdf, rb, kkb, vb, lwb, bb_, kdb, *rest, nbb, zero_init):
    if zero_init:
        yf_ref, yb_ref, sfin_ref, s_scr = rest
    else:
        s0_ref, yf_ref, yb_ref, sfin_ref, s_scr = rest
    c = pl.program_id(1)
    n_pairs = W_R // LANES

    @pl.when(c == 0)
    def _():
        s_scr[...] = jnp.zeros_like(s_scr) if zero_init else s0_ref[...]

    C = CHUNK
    t_i = lax.broadcasted_iota(jnp.int32, (C, C), 0)
    i_i = lax.broadcasted_iota(jnp.int32, (C, C), 1)
    t2 = lax.broadcasted_iota(jnp.int32, (C, LANES), 0)
    i2 = lax.broadcasted_iota(jnp.int32, (C, LANES), 1) & (HD_R - 1)
    same_sub = _idiv(t2, SUB) == _idiv(i2, SUB)
    eye = lax.broadcasted_iota(jnp.int32, (LANES, LANES), 0) == lax.broadcasted_iota(jnp.int32, (LANES, LANES), 1)
    same_head = (_idiv(lax.broadcasted_iota(jnp.int32, (LANES, LANES), 0), HD_R)
                 == _idiv(lax.broadcasted_iota(jnp.int32, (LANES, LANES), 1), HD_R))
    masks = {}
    for rev in (False, True):
        tri = (i_i >= t_i) if rev else (i_i <= t_i)
        masks[rev] = dict(tri=jnp.where(tri, 1.0, 0.0).astype(BF16),
                          strict=(i2 > t2) if rev else (i2 < t2),
                          incl=(i2 >= t2) if rev else (i2 <= t2))

    chains = []
    for bi in range(nbb):
        for rev, (r_, kk_, v_, lw_, b_, kd_) in ((False, (rf, kkf, vf, lwf, bf_, kdf)),
                                                  (True, (rb, kkb, vb, lwb, bb_, kdb))):
            for p in range(n_pairs):
                sl = slice(p * LANES, (p + 1) * LANES)
                chains.append(dict(bi=bi, rev=rev, p=p, r=r_[bi, :, sl], kk=kk_[bi, :, sl], v=v_[bi, :, sl],
                                   lw=lw_[bi, :, sl], b=b_[bi, :, sl], kd=kd_[bi, :, sl]))

    for ch in chains:
        h, m, l = _split3(ch['lw'])
        cl = jnp.dot(masks[ch['rev']]['tri'], jnp.concatenate([h, m, l], axis=1), preferred_element_type=F32)
        ch['cl'] = cl[:, 0:LANES] + cl[:, LANES:2 * LANES] + cl[:, 2 * LANES:3 * LANES]
    for ch in chains:
        cl, lw = ch['cl'], ch['lw']
        p_in = jnp.exp(cl)
        p_inv = jnp.exp(-cl)
        p_ex = jnp.exp(cl - lw)
        ch['pc'] = jnp.exp(jnp.sum(lw, axis=0, keepdims=True))
        p_end = ch['pc'] * p_inv
        ch['at'] = -ch['kk'] * p_ex
        ch['rt'] = ch['r'] * p_in
        ch['bt'] = ch['b'] * p_inv
        ch['kt'] = ch['kd'] * p_inv
        ch['bh'] = ch['b'] * p_end
        ch['kh'] = ch['kd'] * p_end
        ch['S'] = s_scr[ch['bi'], int(ch['rev']), ch['p']]
    for ch in chains:
        g = lax.dot_general(jnp.concatenate([ch['at'], ch['rt']], axis=0).astype(BF16),
                            jnp.concatenate([_pair_rows(ch['bt']), _pair_rows(ch['kt'])], axis=0),
                            NT, preferred_element_type=F32)
        mk = masks[ch['rev']]
        ch['lab'] = jnp.where(mk['strict'], g[0:C, 0:LANES], 0.0)
        ch['lak'] = jnp.where(mk['strict'], g[0:C, LANES:2 * LANES], 0.0)
        ch['lrb'] = jnp.where(mk['incl'], g[C:2 * C, 0:LANES], 0.0)
        ch['lrk'] = jnp.where(mk['incl'], g[C:2 * C, LANES:2 * LANES], 0.0)
        ch['vbd'] = _pair_rows(ch['v'])
    for ch in chains:
        x0 = jnp.dot(jnp.concatenate([ch['at'], ch['lak']], axis=1).astype(BF16),
                     jnp.concatenate([ch['S'].astype(BF16), ch['vbd']], axis=0), preferred_element_type=F32)
        nd = jnp.where(same_sub, ch['lab'], 0.0)
        no = jnp.where(same_sub, 0.0, ch['lab'])
        ch['zc'] = jnp.concatenate([x0, no], axis=1)
        ch['nj'] = nd
    n1 = int(math.log2(SUB))
    for lev in range(n1):
        for ch in chains:
            if lev < n1 - 1:
                rr = jnp.dot(ch['nj'].astype(BF16), _pair_rows(jnp.concatenate([ch['zc'], ch['nj']], axis=1)),
                             preferred_element_type=F32)
                ch['zc'] = ch['zc'] + rr[:, 0:2 * LANES]
                ch['nj'] = rr[:, 2 * LANES:3 * LANES]
            else:
                ch['zc'] = ch['zc'] + jnp.dot(ch['nj'].astype(BF16), _pair_rows(ch['zc']),
                                              preferred_element_type=F32)
    for ch in chains:
        ch['x'] = ch['zc'][:, 0:LANES]
        ch['mj'] = ch['zc'][:, LANES:2 * LANES]
    n2 = int(math.log2(C // SUB))
    for lev in range(n2):
        for ch in chains:
            if lev < n2 - 1:
                rr = jnp.dot(ch['mj'].astype(BF16), _pair_rows(jnp.concatenate([ch['x'], ch['mj']], axis=1)),
                             preferred_element_type=F32)
                ch['x'] = ch['x'] + rr[:, 0:LANES]
                ch['mj'] = rr[:, LANES:2 * LANES]
            else:
                ch['x'] = ch['x'] + jnp.dot(ch['mj'].astype(BF16), _pair_rows(ch['x']), preferred_element_type=F32)
    for ch in chains:
        u = ch['x']
        ubd = _pair_rows(u)
        y = jnp.dot(jnp.concatenate([ch['rt'], ch['lrb'], ch['lrk']], axis=1).astype(BF16),
                    jnp.concatenate([ch['S'].astype(BF16), ubd, ch['vbd']], axis=0), preferred_element_type=F32)
        dg = jnp.where(eye, jnp.broadcast_to(ch['pc'], (LANES, LANES)), 0.0)
        s_new = lax.dot_general(jnp.concatenate([ch['bh'], ch['kh'], dg], axis=0).astype(BF16),
                                jnp.concatenate([u, ch['v'], ch['S']], axis=0).astype(BF16),
                                TN, preferred_element_type=F32)
        s_new = jnp.where(same_head, s_new, 0.0)
        sl = slice(ch['p'] * LANES, (ch['p'] + 1) * LANES)
        if ch['rev']:
            yb_ref[ch['bi'], :, sl] = y
        else:
            yf_ref[ch['bi'], :, sl] = y
        s_scr[ch['bi'], int(ch['rev']), ch['p']] = s_new

    @pl.when(c == pl.num_programs(1) - 1)
    def _():
        sfin_ref[...] = s_scr[...]


def _rwkv_scan(r_src, r_col, v_col, kk, lw, bb, kd, s0, *, nbb):
    nb, t_len, _ = kk.shape
    nc = t_len // CHUNK
    n_pairs = W_R // LANES

    def tok(col, rev):
        return pl.BlockSpec((nbb, CHUNK, W_R), (lambda g, c: (g, nc - 1 - c, col)) if rev else (lambda g, c: (g, c, col)))

    def dirs(z, rev):
        return pl.BlockSpec((None, nbb, CHUNK, W_R),
                            (lambda g, c: (z, g, nc - 1 - c, 0)) if rev else (lambda g, c: (z, g, c, 0)))

    st_spec = pl.BlockSpec((nbb, 2, n_pairs, LANES, LANES), lambda g, c: (g, 0, 0, 0, 0))
    in_specs, args = [], []
    for rev in (False, True):
        z = int(rev)
        in_specs += [tok(r_col, rev), tok(0, rev), tok(v_col, rev), dirs(z, rev), dirs(z, rev), dirs(z, rev)]
        args += [r_src, kk, r_src, lw, bb, kd]
    if s0 is not None:
        in_specs.append(st_spec)
        args.append(s0)
    yshape = jax.ShapeDtypeStruct((nb, t_len, W_R), F32)
    return pl.pallas_call(
        functools.partial(_rwkv_chunk_kernel, nbb=nbb, zero_init=s0 is None),
        grid=(nb // nbb, nc),
        in_specs=in_specs,
        out_specs=[tok(0, False), tok(0, True), st_spec],
        out_shape=[yshape, yshape, jax.ShapeDtypeStruct((nb, 2, n_pairs, LANES, LANES), F32)],
        scratch_shapes=[pltpu.VMEM((nbb, 2, n_pairs, LANES, LANES), F32)],
        compiler_params=_cparams(("parallel", "arbitrary")),
        name="rwkv_scan",
    )(*args)


def _post_kernel(x_ref, yf_ref, yb_ref, bs_ref, vr_ref, gr_ref, oa_ref, ga_ref, gg_ref, mod_ref,
                 lnw_ref, lnb_ref, wa_ref, wr_ref, wo_ref, n2_ref, wi_ref, wo2_ref, o_ref, *, n_chunks):
    d = x_ref.shape[2]
    ones_bd = _group_ones(LANES, HD_R)
    y = yf_ref[0] + yb_ref[0]
    mu = _group_sum(y, ones_bd) * (1.0 / HD_R)
    yc = y - mu
    var = _group_sum(yc * yc, ones_bd) * (1.0 / HD_R)
    yn = yc * lax.rsqrt(var + EPS_GN) * lnw_ref[...] + lnb_ref[...]
    o_r = (yn + bs_ref[0] * vr_ref[0]) * _sigmoid(gr_ref[0])
    merged = ga_ref[0] * _dot(oa_ref[0], wa_ref[...]) + gg_ref[0] * _dot(o_r, wr_ref[...])
    g1 = mod_ref[0, :, 2 * d:3 * d]
    x1 = x_ref[0] + g1 * _dot(merged, wo_ref[...])
    sh2 = mod_ref[0, :, 3 * d:4 * d]
    sc2 = mod_ref[0, :, 4 * d:5 * d]
    ms = jnp.mean(x1 * x1, axis=-1, keepdims=True)
    h = (x1 * lax.rsqrt(ms + EPS_RMS) * n2_ref[...] * (1.0 + sc2) + sh2).astype(BF16)

    d_ff = wo2_ref.shape[0]
    tf = d_ff // n_chunks

    def up(c):
        u = jnp.dot(h, wi_ref[:, c * tf:(c + 1) * tf], preferred_element_type=F32)
        g = jnp.dot(h, wi_ref[:, d_ff + c * tf:d_ff + (c + 1) * tf], preferred_element_type=F32)
        return u, g

    acc = None
    nxt = up(0)
    for c in range(n_chunks):
        u, g = nxt
        if c + 1 < n_chunks:
            nxt = up(c + 1)
        a = ((u * _sigmoid(u)) * g).astype(BF16)
        part = jnp.dot(a, wo2_ref[c * tf:(c + 1) * tf, :], preferred_element_type=F32)
        acc = part if acc is None else acc + part
    o_ref[0] = x1 + mod_ref[0, :, 5 * d:6 * d] * acc


def _post(x, zr, gates, yf, yb, bsum, oa, mod, ln_x_w, ln_x_b, wa_bf, wr_bf, wo_bf, norm2_w, w_in_bf, w_out_bf,
          *, tm, n_chunks):
    nb, t_len, d = x.shape
    tok = lambda w, col: pl.BlockSpec((1, tm, w), lambda b, i: (b, i, col))
    resident = lambda a: pl.BlockSpec(a.shape, lambda b, i: (0,) * a.ndim, pipeline_mode=pl.Buffered(1))
    lnw, lnb, n2 = ln_x_w.reshape(1, W_R), ln_x_b.reshape(1, W_R), norm2_w.reshape(1, d)
    return pl.pallas_call(
        functools.partial(_post_kernel, n_chunks=n_chunks),
        grid=(nb, t_len // tm),
        in_specs=[tok(d, 0), tok(W_R, 0), tok(W_R, 0), tok(W_R, 0), tok(COL, ZR_VR), tok(COL, ZR_GR),
                  tok(W_R, 0), tok(d, 0), tok(d, 1),
                  pl.BlockSpec((1, 1, mod.shape[2]), lambda b, i: (b, 0, 0)),
                  resident(lnw), resident(lnb), resident(wa_bf), resident(wr_bf), resident(wo_bf), resident(n2),
                  resident(w_in_bf), resident(w_out_bf)],
        out_specs=tok(d, 0),
        out_shape=jax.ShapeDtypeStruct((nb, t_len, d), F32),
        compiler_params=pltpu.CompilerParams(dimension_semantics=("parallel", "parallel"),
                                             vmem_limit_bytes=INPROJ_VMEM_LIMIT),
        name="post",
    )(x, yf, yb, bsum, zr, zr, oa, gates, gates, mod, lnw, lnb, wa_bf, wr_bf, wo_bf, n2, w_in_bf, w_out_bf)


def _state_to_pairs(s):
    nb = s.shape[0]
    st = jnp.swapaxes(s, -1, -2).reshape(nb, 2, N_HEADS_R // 2, 2, HD_R, HD_R)
    zero = jnp.zeros_like(st[:, :, :, 0])
    top = jnp.concatenate([st[:, :, :, 0], zero], axis=-1)
    bot = jnp.concatenate([zero, st[:, :, :, 1]], axis=-1)
    return jnp.concatenate([top, bot], axis=-2)


def _pairs_to_state(sp):
    nb = sp.shape[0]
    h0 = sp[:, :, :, 0:HD_R, 0:HD_R]
    h1 = sp[:, :, :, HD_R:, HD_R:]
    st = jnp.stack([h0, h1], axis=3).reshape(nb, 2, N_HEADS_R, HD_R, HD_R)
    return jnp.swapaxes(st, -1, -2)


def _layer(x_tok, nb_seq, mod, lp, lam_init, cache, s0_pairs, rope, *, tm, tq, nbb):
    nbm, tmod, d = x_tok.shape
    t_seq = nbm * tmod // nb_seq
    outs = _inproj(x_tok, mod, lp['norm1_w'], lp['w_in'], lp['qkw'], rope, tm=tm)
    qkv, zr, gates = outs[:3]
    kv = outs[3:5] if cache is None else None
    seq = lambda a: a.reshape(a.shape[:-3] + (nb_seq, t_seq, a.shape[-1]))
    oa = _attention(seq(qkv), cache, lp['lam'], lp['subln_w'], lam_init, tq=tq,
                    heads_per_step=1 if cache is not None else N_HEADS_A)
    kk, lw, bb, kd, bsum = _rwkv_prep(zr, lp['w_lora_up'], lp['a_lora_up'], lp['w0'], lp['a0'],
                                      lp['k_k'], lp['k_a'], lp['r_k'], tm=512)
    yf, yb, s_fin = _rwkv_scan(seq(zr), ZR_RR, ZR_VR, seq(kk), seq(lw), seq(bb), seq(kd), s0_pairs, nbb=nbb)
    tokv = lambda a: a.reshape(nbm, tmod, a.shape[-1])
    y = _post(x_tok, zr, gates, tokv(yf), tokv(yb), bsum, tokv(oa), mod, lp['ln_x_w'], lp['ln_x_b'],
              lp['w_attn_br'], lp['w_rwkv_br'], lp['w_out'], lp['norm2_w'], lp['w_ffn_in'], lp['w_ffn_out'],
              tm=256, n_chunks=2)
    return y, kv, s_fin


def kernel(x_prompt, x_sample, cache_k, cache_v, state_rwkv, c, c_ctx, ada_w, ada_b, norm1_w, norm2_w, w_in, q_norm_w, k_norm_w, lambda_q1, lambda_k1, lambda_q2, lambda_k2, subln_w, w_lora_up, w0, a_lora_up, a0, k_k, k_a, r_k, ln_x_w, ln_x_b, w_attn_br, w_rwkv_br, w_out, w_ffn_in, w_ffn_out):
    depth = ada_w.shape[0]
    batch, seq_len, d = x_prompt.shape
    dec_batch, dec_seq, _ = x_sample.shape
    past = cache_k.shape[2]
    qk_cols = N_HEADS_A * 2 * HD_A
    assert qk_cols == COL and W_R == COL and N_HEADS_A * VD_A == COL

    rope = _rope_tables(dec_seq)
    y_prompt = x_prompt.reshape(1, batch * seq_len, d)
    y_sample = x_sample
    ks_out, vs_out, ss_out = [], [], []
    cond_rows = 16
    for li in range(depth):
        cond = jnp.concatenate([c_ctx[None, :], c, jnp.zeros((cond_rows - 1 - dec_batch, d), F32)], axis=0)
        m = _modulation(cond, ada_w[li], ada_b[li])
        mod_ctx = m[0:1].reshape(1, 1, 6 * d)
        mod_lat = m[1:1 + dec_batch].reshape(dec_batch, 1, 6 * d)

        wi = w_in[li]
        o_lora = 3 * COL + 4 * W_R
        o_gate = o_lora + 4 * LORA
        w_re = jnp.concatenate([wi[:, :o_lora], wi[:, o_lora:o_gate], jnp.zeros((d, COL - 4 * LORA), F32),
                                wi[:, o_gate:]], axis=1).astype(BF16)
        assert w_re.shape[1] == N_TILES * COL
        lp = {
            'norm1_w': norm1_w[li], 'norm2_w': norm2_w[li], 'w_in': w_re,
            'qkw': jnp.stack([jnp.tile(q_norm_w[li], 2 * N_HEADS_A), jnp.tile(k_norm_w[li], 2 * N_HEADS_A)]),
            'lam': jnp.stack([lambda_q1[li], lambda_k1[li], lambda_q2[li], lambda_k2[li]]),
            'subln_w': subln_w[li], 'w_lora_up': w_lora_up[li], 'w0': w0[li], 'a_lora_up': a_lora_up[li],
            'a0': a0[li], 'k_k': k_k[li], 'k_a': k_a[li], 'r_k': r_k[li], 'ln_x_w': ln_x_w[li],
            'ln_x_b': ln_x_b[li], 'w_attn_br': w_attn_br[li].astype(BF16), 'w_rwkv_br': w_rwkv_br[li].astype(BF16),
            'w_out': w_out[li].astype(BF16), 'w_ffn_in': w_ffn_in[li].astype(BF16),
            'w_ffn_out': w_ffn_out[li].astype(BF16),
        }
        lam_init = 0.8 - 0.6 * math.exp(-0.3 * li)

        y_prompt, kv_ctx, s_ctx = _layer(y_prompt, batch, mod_ctx, lp, lam_init, None, None, None,
                                         tm=512, tq=seq_len, nbb=2)
        ks_out.append(kv_ctx[0].reshape(batch, seq_len, N_HEADS_A, 2, HD_A))
        vs_out.append(kv_ctx[1].reshape(batch, seq_len, N_HEADS_A, VD_A))
        ss_out.append(_pairs_to_state(s_ctx))

        cache = (cache_k[:, li].reshape(dec_batch, past, COL).astype(BF16),
                 cache_v[:, li].reshape(dec_batch, past, COL).astype(BF16))
        s0_lat = _state_to_pairs(state_rwkv[:, li])
        y_sample, _, _ = _layer(y_sample, dec_batch, mod_lat, lp, lam_init, cache, s0_lat, rope,
                                tm=512, tq=512, nbb=4)
    new_k = jnp.stack(ks_out, axis=1)
    new_v = jnp.stack(vs_out, axis=1)
    new_state = jnp.stack(ss_out, axis=1)
    return (y_prompt.reshape(batch, seq_len, d), y_sample, new_k, new_v, new_state)
```

```python
import functools
import math

import jax
import jax.numpy as jnp
from jax import lax
from jax.experimental import pallas as pl
from jax.experimental.pallas import tpu as pltpu

F32 = jnp.float32
BF16 = jnp.bfloat16

N_HEADS_A = 4
HD_A = 64
VD_A = 2 * HD_A
N_HEADS_R = 8
HD_R = 64
W_R = N_HEADS_R * HD_R
LORA = 64
GRID_W = 64
ROPE_THETA = 10000.0
EPS_RMS = 1e-6
EPS_GN = 64e-5

LANES = 128
VMEM_LIMIT = 48 * 1024 * 1024
INPROJ_VMEM_LIMIT = 58 * 1024 * 1024

CHUNK = 64
SUB = 8
COL = 512

NN = (((1,), (0,)), ((), ()))
NT = (((1,), (1,)), ((), ()))
TN = (((0,), (0,)), ((), ()))


def _dot(a, b, dims=NN):
    return lax.dot_general(a.astype(BF16), b.astype(BF16), dims, preferred_element_type=F32)


def _split2(x):
    hi = x.astype(BF16)
    lo = (x - hi.astype(F32)).astype(BF16)
    return hi, lo


def _split3(x):
    hi = x.astype(BF16)
    r1 = x - hi.astype(F32)
    mid = r1.astype(BF16)
    lo = (r1 - mid.astype(F32)).astype(BF16)
    return hi, mid, lo


def _dot_x3(a, b, dims=NN):
    ah, al = _split2(a)
    bh, bl = _split2(b)
    d = functools.partial(lax.dot_general, dimension_numbers=dims, preferred_element_type=F32)
    return d(ah, bh) + d(ah, bl) + d(al, bh)


def _sigmoid(x):
    return 1.0 / (1.0 + jnp.exp(-x))


def _idiv(x, pow2):
    assert pow2 & (pow2 - 1) == 0
    return x >> int(math.log2(pow2))


def _group_ones(n, group):
    r = _idiv(lax.broadcasted_iota(jnp.int32, (n, n), 0), group)
    c = _idiv(lax.broadcasted_iota(jnp.int32, (n, n), 1), group)
    return jnp.where(r == c, 1.0, 0.0).astype(BF16)


def _group_sum(x, ones_bd):
    ones2 = jnp.concatenate([ones_bd, ones_bd], axis=0)
    outs = []
    for cb in range(x.shape[1] // LANES):
        hi, lo = _split2(x[:, cb * LANES:(cb + 1) * LANES])
        outs.append(jnp.dot(jnp.concatenate([hi, lo], axis=1), ones2, preferred_element_type=F32))
    return outs[0] if len(outs) == 1 else jnp.concatenate(outs, axis=1)


def _half_block_sum(x):
    half = LANES // 2
    outs = []
    for cb in range(x.shape[1] // LANES):
        xb = x[:, cb * LANES:(cb + 1) * LANES]
        low = lax.broadcasted_iota(jnp.int32, xb.shape, 1) < half
        s_lo = jnp.sum(jnp.where(low, xb, 0.0), axis=-1, keepdims=True)
        s_hi = jnp.sum(jnp.where(low, 0.0, xb), axis=-1, keepdims=True)
        outs.append(jnp.where(low, s_lo, s_hi))
    return outs[0] if len(outs) == 1 else jnp.concatenate(outs, axis=1)


def _cparams(sem):
    return pltpu.CompilerParams(dimension_semantics=sem, vmem_limit_bytes=VMEM_LIMIT)


def _mod_kernel(c_ref, w_ref, b_ref, o_ref):
    c = c_ref[...]
    s = c * _sigmoid(c)
    o_ref[...] = _dot_x3(s, w_ref[...]) + b_ref[...]


def _modulation(cond, ada_w, ada_b):
    rows, d = cond.shape
    n = ada_w.shape[1]
    tn = 1536
    return pl.pallas_call(
        _mod_kernel,
        grid=(n // tn,),
        in_specs=[pl.BlockSpec((rows, d), lambda j: (0, 0)),
                  pl.BlockSpec((d, tn), lambda j: (0, j)),
                  pl.BlockSpec((1, tn), lambda j: (0, j))],
        out_specs=pl.BlockSpec((rows, tn), lambda j: (0, j)),
        out_shape=jax.ShapeDtypeStruct((rows, n), F32),
        compiler_params=_cparams(("parallel",)),
        name="modulation",
    )(cond, ada_w, ada_b.reshape(1, n))


def _rope_kernel(cos_ref, sin_ref, *, tm):
    i = pl.program_id(0)
    shift = int(math.log2(GRID_W))
    t = i * tm + lax.broadcasted_iota(jnp.int32, (tm, LANES), 0)
    lane = lax.broadcasted_iota(jnp.int32, (tm, LANES), 1)
    l64 = lane & (HD_A - 1)
    nf = HD_A // 4
    f = (l64 & (nf - 1)).astype(F32)
    inv = jnp.exp(f * (-math.log(ROPE_THETA) / nf))
    pos = jnp.where(l64 < HD_A // 2, t >> shift, t & (GRID_W - 1)).astype(F32)
    ang = pos * inv
    sn = jnp.sin(ang)
    cos_ref[...] = jnp.cos(ang)
    sin_ref[...] = jnp.where((l64 & (2 * nf - 1)) < nf, -sn, sn)


def _rope_tables(t_len):
    assert GRID_W & (GRID_W - 1) == 0
    tm = 512
    return pl.pallas_call(
        functools.partial(_rope_kernel, tm=tm),
        grid=(t_len // tm,),
        out_specs=[pl.BlockSpec((tm, LANES), lambda i: (i, 0))] * 2,
        out_shape=[jax.ShapeDtypeStruct((t_len, LANES), F32)] * 2,
        compiler_params=_cparams(("parallel",)),
        name="rope_tables",
    )()


TILE_Q, TILE_K, TILE_V, TILE_RR, TILE_KR, TILE_VR, TILE_GR, TILE_LORA, TILE_GATE = 0, 1, 2, 3, 4, 5, 6, 7, 8
N_TILES = 12
ZR_RR, ZR_KR, ZR_VR, ZR_GR, ZR_LORA = 0, 1, 2, 3, 4
N_ZR = 5
N_GATE = N_TILES - TILE_GATE
Q_SCALE = HD_A ** -0.5 * math.log2(math.e)


def _rope_apply(y, cos, sin):
    n = y.shape[1]
    lane = lax.broadcasted_iota(jnp.int32, y.shape, 1)
    quarter = HD_A // 4
    first = (lane & (2 * quarter - 1)) < quarter
    swapped = jnp.where(first, pltpu.roll(y, n - quarter, 1), pltpu.roll(y, quarter, 1))
    reps = n // LANES
    cos_f = jnp.concatenate([cos] * reps, axis=1)
    sin_f = jnp.concatenate([sin] * reps, axis=1)
    return y * cos_f + swapped * sin_f


def _inproj_kernel(*refs, latent):
    if latent:
        x_ref, mod_ref, nw_ref, w_ref, qkw_ref, cos_ref, sin_ref, qkv_ref, zr_ref, gate_ref = refs
    else:
        x_ref, mod_ref, nw_ref, w_ref, qkw_ref, qkv_ref, zr_ref, gate_ref, k32_ref, v32_ref = refs
    d = x_ref.shape[2]
    x = x_ref[0]
    ms = jnp.mean(x * x, axis=-1, keepdims=True)
    y = x * lax.rsqrt(ms + EPS_RMS) * nw_ref[...]
    h = (y * (1.0 + mod_ref[0, :, d:2 * d]) + mod_ref[0, :, 0:d]).astype(BF16)

    def tile(j):
        return jnp.dot(h, w_ref[:, j * COL:(j + 1) * COL], preferred_element_type=F32)

    def qk_norm(zz, w_row):
        gms = _half_block_sum(zz * zz) * (1.0 / HD_A)
        return zz * lax.rsqrt(gms + EPS_RMS) * w_row

    def maybe_rope(v):
        return _rope_apply(v, cos_ref[...], sin_ref[...]) if latent else v

    def col(j):
        return slice(j * COL, (j + 1) * COL)

    order = [TILE_Q, TILE_K, TILE_V] + list(range(TILE_GATE, N_TILES)) + list(range(TILE_RR, TILE_GATE))
    z_next = tile(order[0])
    for pos, j in enumerate(order):
        z = z_next
        if pos + 1 < N_TILES:
            z_next = tile(order[pos + 1])
        if j == TILE_Q:
            qkv_ref[0, :, col(0)] = (maybe_rope(qk_norm(z, qkw_ref[0:1, :])) * Q_SCALE).astype(BF16)
        elif j == TILE_K:
            kn = qk_norm(z, qkw_ref[1:2, :])
            qkv_ref[0, :, col(1)] = maybe_rope(kn).astype(BF16)
            if not latent:
                k32_ref[0] = kn
        elif j == TILE_V:
            qkv_ref[0, :, col(2)] = z.astype(BF16)
            if not latent:
                v32_ref[0] = z
        elif j < TILE_GATE:
            zr_ref[0, :, col(j - TILE_RR)] = z
        else:
            gate_ref[0, :, col(j - TILE_GATE)] = _sigmoid(z)


def _inproj(x, mod, norm1_w, w_bf, qkw, rope, *, tm):
    nb, t_len, d = x.shape
    latent = rope is not None
    ncols = w_bf.shape[1]
    in_specs = [pl.BlockSpec((1, tm, d), lambda b, i: (b, i, 0)),
                pl.BlockSpec((1, 1, mod.shape[2]), lambda b, i: (b, 0, 0)),
                pl.BlockSpec((1, d), lambda b, i: (0, 0)),
                pl.BlockSpec((d, ncols), lambda b, i: (0, 0), pipeline_mode=pl.Buffered(1)),
                pl.BlockSpec((2, COL), lambda b, i: (0, 0))]
    args = [x, mod, norm1_w.reshape(1, d), w_bf, qkw]
    if latent:
        in_specs += [pl.BlockSpec((tm, LANES), lambda b, i: (i, 0))] * 2
        args += list(rope)
    widths = [3, N_ZR, N_GATE] + ([] if latent else [1, 1])
    dtypes = [BF16, F32, F32] + ([] if latent else [F32, F32])
    out_specs = [pl.BlockSpec((1, tm, n * COL), lambda b, i: (b, i, 0)) for n in widths]
    out_shape = [jax.ShapeDtypeStruct((nb, t_len, n * COL), dt) for n, dt in zip(widths, dtypes)]
    return pl.pallas_call(
        functools.partial(_inproj_kernel, latent=latent),
        grid=(nb, t_len // tm),
        in_specs=in_specs,
        out_specs=out_specs,
        out_shape=out_shape,
        compiler_params=pltpu.CompilerParams(dimension_semantics=("parallel", "parallel"),
                                             vmem_limit_bytes=INPROJ_VMEM_LIMIT),
        name="inproj_latent" if latent else "inproj_context",
    )(*args)


KV_CHUNK = 512


def _attn_head(q, k_ref, v_ref, cache_refs, hs, lam, sw, lam_init):
    tq = q.shape[0]
    t_k = k_ref.shape[1]
    lane = lax.broadcasted_iota(jnp.int32, q.shape, 1)
    qs = jnp.concatenate([jnp.where(lane < HD_A, q, 0.0), jnp.where(lane >= HD_A, q, 0.0)], axis=0).astype(BF16)

    chunks = []
    if cache_refs is not None:
        chunks.append((cache_refs[0][0, :, hs], cache_refs[1][0, :, hs]))
    kc = min(KV_CHUNK, t_k)
    for c in range(t_k // kc):
        chunks.append((k_ref[0, c * kc:(c + 1) * kc, hs], v_ref[0, c * kc:(c + 1) * kc, hs]))

    def scores(c):
        return lax.dot_general(qs, chunks[c][0], NT, preferred_element_type=F32)

    m = acc = None
    s_next = scores(0)
    for c, (_, v_c) in enumerate(chunks):
        s = s_next
        if c + 1 < len(chunks):
            s_next = scores(c + 1)
        mx = jnp.max(s, axis=-1, keepdims=True)
        m_new = mx if m is None else jnp.maximum(m, mx)
        p = jnp.exp2(s - m_new)
        v_aug = jnp.concatenate([v_c, jnp.ones_like(v_c)], axis=1)
        pv = jnp.dot(p.astype(BF16), v_aug, preferred_element_type=F32)
        acc = pv if m is None else jnp.exp2(m - m_new) * acc + pv
        m = m_new
    o = acc[:, 0:VD_A] / acc[:, VD_A:2 * VD_A]
    o = o[0:tq] - lam * o[tq:2 * tq]
    ms = jnp.mean(o * o, axis=-1, keepdims=True)
    return o * lax.rsqrt(ms + EPS_RMS) * sw * (1.0 - lam_init)


def _attn_kernel(*refs, has_cache, lam_init):
    if has_cache:
        q_ref, k_ref, v_ref, ck_ref, cv_ref, lam_ref, sw_ref, o_ref = refs
        cache_refs = (ck_ref, cv_ref)
    else:
        q_ref, k_ref, v_ref, lam_ref, sw_ref, o_ref = refs
        cache_refs = None
    lp = lam_ref[...]
    lam = (jnp.exp(jnp.sum(lp[0:1] * lp[1:2], axis=-1, keepdims=True))
           - jnp.exp(jnp.sum(lp[2:3] * lp[3:4], axis=-1, keepdims=True)) + lam_init)
    for h in range(q_ref.shape[2] // LANES):
        hs = slice(h * LANES, (h + 1) * LANES)
        o = _attn_head(q_ref[0, :, hs].astype(F32), k_ref, v_ref, cache_refs, hs, lam, sw_ref[...], lam_init)
        o_ref[0, :, hs] = o.astype(o_ref.dtype)


def _attention(qkv, cache, lam_p, subln_w, lam_init, *, tq, heads_per_step):
    nb, t_len, _ = qkv.shape
    hw = heads_per_step * LANES
    n_hg = N_HEADS_A // heads_per_step
    in_specs = [pl.BlockSpec((1, tq, hw), lambda b, h, i: (b, i, h)),
                pl.BlockSpec((1, t_len, hw), lambda b, h, i: (b, 0, n_hg + h)),
                pl.BlockSpec((1, t_len, hw), lambda b, h, i: (b, 0, 2 * n_hg + h))]
    args = [qkv, qkv, qkv]
    if cache is not None:
        ck, cv = cache
        past = ck.shape[1]
        in_specs += [pl.BlockSpec((1, past, hw), lambda b, h, i: (b, 0, h))] * 2
        args += [ck, cv]
    in_specs += [pl.BlockSpec((4, HD_A), lambda b, h, i: (0, 0)),
                 pl.BlockSpec((1, VD_A), lambda b, h, i: (0, 0))]
    args += [lam_p, subln_w.reshape(1, VD_A)]
    return pl.pallas_call(
        functools.partial(_attn_kernel, has_cache=cache is not None, lam_init=lam_init),
        grid=(nb, n_hg, t_len // tq),
        in_specs=in_specs,
        out_specs=pl.BlockSpec((1, tq, hw), lambda b, h, i: (b, i, h)),
        out_shape=jax.ShapeDtypeStruct((nb, t_len, N_HEADS_A * VD_A), BF16),
        compiler_params=_cparams(("parallel", "parallel", "arbitrary")),
        name="attention_latent" if cache is not None else "attention_context",
    )(*args)


def _prep_kernel(r_ref, k_ref, la_ref, wup_ref, aup_ref, w0_ref, a0_ref, kk_ref_p, ka_ref, rk_ref,
                 kk_out, lw_out, b_out, kd_out, bs_out):
    r = r_ref[0]
    k = k_ref[0]
    la = la_ref[0]
    ones_bd = _group_ones(LANES, HD_R)
    kraw = k * kk_ref_p[...]
    nrm = jnp.sqrt(_group_sum(kraw * kraw, ones_bd))
    kk = kraw / jnp.maximum(nrm, 1e-12)
    kk_out[0] = kk
    wl = jnp.tanh(la[:, 0:2 * LORA])
    al = la[:, 2 * LORA:4 * LORA]
    lane = lax.broadcasted_iota(jnp.int32, wl.shape, 1)
    rkd = None
    for z in range(2):
        sel = _idiv(lane, LORA) == z
        w = w0_ref[z:z + 1, :] + _dot_x3(jnp.where(sel, wl, 0.0), wup_ref[...])
        lw_out[z, 0] = -math.exp(-0.5) * _sigmoid(w)
        a = _sigmoid(a0_ref[z:z + 1, :] + _dot_x3(jnp.where(sel, al, 0.0), aup_ref[...]))
        kd = k * (1.0 + (a - 1.0) * ka_ref[...])
        b_out[z, 0] = kk * a
        kd_out[z, 0] = kd
        rkd = r * kd if rkd is None else rkd + r * kd
    bs_out[0] = _group_sum(rkd * rk_ref[...], ones_bd)


def _rwkv_prep(z, w_lora_up, a_lora_up, w0, a0, k_k, k_a, r_k, *, tm):
    nb, t_len, _ = z.shape
    tok = lambda col: pl.BlockSpec((1, tm, COL), lambda b, i: (b, i, col))
    par = lambda rows: pl.BlockSpec((rows, W_R), lambda b, i: (0, 0))
    dir_spec = pl.BlockSpec((2, 1, tm, W_R), lambda b, i: (0, b, i, 0))
    tok_out = pl.BlockSpec((1, tm, W_R), lambda b, i: (b, i, 0))
    one = jax.ShapeDtypeStruct((nb, t_len, W_R), F32)
    two = jax.ShapeDtypeStruct((2, nb, t_len, W_R), F32)
    return pl.pallas_call(
        _prep_kernel,
        grid=(nb, t_len // tm),
        in_specs=[tok(ZR_RR), tok(ZR_KR),
                  pl.BlockSpec((1, tm, 4 * LORA), lambda b, i: (b, i, ZR_LORA * COL // (4 * LORA))),
                  par(2 * LORA), par(2 * LORA), par(2), par(2), par(1), par(1), par(1)],
        out_specs=[tok_out, dir_spec, dir_spec, dir_spec, tok_out],
        out_shape=[one, two, two, two, one],
        compiler_params=_cparams(("parallel", "parallel")),
        name="rwkv_prep",
    )(z, z, z, w_lora_up.reshape(2 * LORA, W_R), a_lora_up.reshape(2 * LORA, W_R), w0, a0,
      k_k.reshape(1, W_R), k_a.reshape(1, W_R), r_k.reshape(1, W_R))


def _pair_rows(y):
    lane = lax.broadcasted_iota(jnp.int32, y.shape, 1) & (LANES - 1)
    return jnp.concatenate([jnp.where(lane < HD_R, y, 0.0).astype(BF16),
                            jnp.where(lane >= HD_R, y, 0.0).astype(BF16)], axis=0)


def _rwkv_chunk_kernel(rf, kkf, vf, lwf, bf_, kdf, rb, kkb, vb, lwb, bb_, kdb, *rest, nbb, zero_init):
    if zero_init:
        yf_ref, yb_ref, sfin_ref, s_scr = rest
    else:
        s0_ref, yf_ref, yb_ref, sfin_ref, s_scr = rest
    c = pl.program_id(1)
    n_pairs = W_R // LANES

    @pl.when(c == 0)
    def _():
        s_scr[...] = jnp.zeros_like(s_scr) if zero_init else s0_ref[...]

    C = CHUNK
    t_i = lax.broadcasted_iota(jnp.int32, (C, C), 0)
    i_i = lax.broadcasted_iota(jnp.int32, (C, C), 1)
    t2 = lax.broadcasted_iota(jnp.int32, (C, LANES), 0)
    i2 = lax.broadcasted_iota(jnp.int32, (C, LANES), 1) & (HD_R - 1)
    same_sub = _idiv(t2, SUB) == _idiv(i2, SUB)
    eye = lax.broadcasted_iota(jnp.int32, (LANES, LANES), 0) == lax.broadcasted_iota(jnp.int32, (LANES, LANES), 1)
    same_head = (_idiv(lax.broadcasted_iota(jnp.int32, (LANES, LANES), 0), HD_R)
                 == _idiv(lax.broadcasted_iota(jnp.int32, (LANES, LANES), 1), HD_R))
    masks = {}
    for rev in (False, True):
        tri = (i_i >= t_i) if rev else (i_i <= t_i)
        masks[rev] = dict(tri=jnp.where(tri, 1.0, 0.0).astype(BF16),
                          strict=(i2 > t2) if rev else (i2 < t2),
                          incl=(i2 >= t2) if rev else (i2 <= t2))

    chains = []
    for bi in range(nbb):
        for rev, (r_, kk_, v_, lw_, b_, kd_) in ((False, (rf, kkf, vf, lwf, bf_, kdf)),
                                                  (True, (rb, kkb, vb, lwb, bb_, kdb))):
            for p in range(n_pairs):
                sl = slice(p * LANES, (p + 1) * LANES)
                chains.append(dict(bi=bi, rev=rev, p=p, r=r_[bi, :, sl], kk=kk_[bi, :, sl], v=v_[bi, :, sl],
                                   lw=lw_[bi, :, sl], b=b_[bi, :, sl], kd=kd_[bi, :, sl]))

    for ch in chains:
        h, m, l = _split3(ch['lw'])
        cl = jnp.dot(masks[ch['rev']]['tri'], jnp.concatenate([h, m, l], axis=1), preferred_element_type=F32)
        ch['cl'] = cl[:, 0:LANES] + cl[:, LANES:2 * LANES] + cl[:, 2 * LANES:3 * LANES]
    for ch in chains:
        cl, lw = ch['cl'], ch['lw']
        p_in = jnp.exp(cl)
        p_inv = jnp.exp(-cl)
        p_ex = jnp.exp(cl - lw)
        ch['pc'] = jnp.exp(jnp.sum(lw, axis=0, keepdims=True))
        p_end = ch['pc'] * p_inv
        ch['at'] = -ch['kk'] * p_ex
        ch['rt'] = ch['r'] * p_in
        ch['bt'] = ch['b'] * p_inv
        ch['kt'] = ch['kd'] * p_inv
        ch['bh'] = ch['b'] * p_end
        ch['kh'] = ch['kd'] * p_end
        ch['S'] = s_scr[ch['bi'], int(ch['rev']), ch['p']]
    for ch in chains:
        g = lax.dot_general(jnp.concatenate([ch['at'], ch['rt']], axis=0).astype(BF16),
                            jnp.concatenate([_pair_rows(ch['bt']), _pair_rows(ch['kt'])], axis=0),
                            NT, preferred_element_type=F32)
        mk = masks[ch['rev']]
        ch['lab'] = jnp.where(mk['strict'], g[0:C, 0:LANES], 0.0)
        ch['lak'] = jnp.where(mk['strict'], g[0:C, LANES:2 * LANES], 0.0)
        ch['lrb'] = jnp.where(mk['incl'], g[C:2 * C, 0:LANES], 0.0)
        ch['lrk'] = jnp.where(mk['incl'], g[C:2 * C, LANES:2 * LANES], 0.0)
        ch['vbd'] = _pair_rows(ch['v'])
    for ch in chains:
        x0 = jnp.dot(jnp.concatenate([ch['at'], ch['lak']], axis=1).astype(BF16),
                     jnp.concatenate([ch['S'].astype(BF16), ch['vbd']], axis=0), preferred_element_type=F32)
        nd = jnp.where(same_sub, ch['lab'], 0.0)
        no = jnp.where(same_sub, 0.0, ch['lab'])
        ch['zc'] = jnp.concatenate([x0, no], axis=1)
        ch['nj'] = nd
    n1 = int(math.log2(SUB))
    for lev in range(n1):
        for ch in chains:
            if lev < n1 - 1:
                rr = jnp.dot(ch['nj'].astype(BF16), _pair_rows(jnp.concatenate([ch['zc'], ch['nj']], axis=1)),
                             preferred_element_type=F32)
                ch['zc'] = ch['zc'] + rr[:, 0:2 * LANES]
                ch['nj'] = rr[:, 2 * LANES:3 * LANES]
            else:
                ch['zc'] = ch['zc'] + jnp.dot(ch['nj'].astype(BF16), _pair_rows(ch['zc']),
                                              preferred_element_type=F32)
    for ch in chains:
        ch['x'] = ch['zc'][:, 0:LANES]
        ch['mj'] = ch['zc'][:, LANES:2 * LANES]
    n2 = int(math.log2(C // SUB))
    for lev in range(n2):
        for ch in chains:
            if lev < n2 - 1:
                rr = jnp.dot(ch['mj'].astype(BF16), _pair_rows(jnp.concatenate([ch['x'], ch['mj']], axis=1)),
                             preferred_element_type=F32)
                ch['x'] = ch['x'] + rr[:, 0:LANES]
                ch['mj'] = rr[:, LANES:2 * LANES]
            else:
                ch['x'] = ch['x'] + jnp.dot(ch['mj'].astype(BF16), _pair_rows(ch['x']), preferred_element_type=F32)
    for ch in chains:
        u = ch['x']
        ubd = _pair_rows(u)
        y = jnp.dot(jnp.concatenate([ch['rt'], ch['lrb'], ch['lrk']], axis=1).astype(BF16),
                    jnp.concatenate([ch['S'].astype(BF16), ubd, ch['vbd']], axis=0), preferred_element_type=F32)
        dg = jnp.where(eye, jnp.broadcast_to(ch['pc'], (LANES, LANES)), 0.0)
        s_new = lax.dot_general(jnp.concatenate([ch['bh'], ch['kh'], dg], axis=0).astype(BF16),
                                jnp.concatenate([u, ch['v'], ch['S']], axis=0).astype(BF16),
                                TN, preferred_element_type=F32)
        s_new = jnp.where(same_head, s_new, 0.0)
        sl = slice(ch['p'] * LANES, (ch['p'] + 1) * LANES)
        if ch['rev']:
            yb_ref[ch['bi'], :, sl] = y
        else:
            yf_ref[ch['bi'], :, sl] = y
        s_scr[ch['bi'], int(ch['rev']), ch['p']] = s_new

    @pl.when(c == pl.num_programs(1) - 1)
    def _():
        sfin_ref[...] = s_scr[...]


def _rwkv_scan(r_src, r_col, v_col, kk, lw, bb, kd, s0, *, nbb):
    nb, t_len, _ = kk.shape
    nc = t_len // CHUNK
    n_pairs = W_R // LANES

    def tok(col, rev):
        return pl.BlockSpec((nbb, CHUNK, W_R), (lambda g, c: (g, nc - 1 - c, col)) if rev else (lambda g, c: (g, c, col)))

    def dirs(z, rev):
        return pl.BlockSpec((None, nbb, CHUNK, W_R),
                            (lambda g, c: (z, g, nc - 1 - c, 0)) if rev else (lambda g, c: (z, g, c, 0)))

    st_spec = pl.BlockSpec((nbb, 2, n_pairs, LANES, LANES), lambda g, c: (g, 0, 0, 0, 0))
    in_specs, args = [], []
    for rev in (False, True):
        z = int(rev)
        in_specs += [tok(r_col, rev), tok(0, rev), tok(v_col, rev), dirs(z, rev), dirs(z, rev), dirs(z, rev)]
        args += [r_src, kk, r_src, lw, bb, kd]
    if s0 is not None:
        in_specs.append(st_spec)
        args.append(s0)
    yshape = jax.ShapeDtypeStruct((nb, t_len, W_R), F32)
    return pl.pallas_call(
        functools.partial(_rwkv_chunk_kernel, nbb=nbb, zero_init=s0 is None),
        grid=(nb // nbb, nc),
        in_specs=in_specs,
        out_specs=[tok(0, False), tok(0, True), st_spec],
        out_shape=[yshape, yshape, jax.ShapeDtypeStruct((nb, 2, n_pairs, LANES, LANES), F32)],
        scratch_shapes=[pltpu.VMEM((nbb, 2, n_pairs, LANES, LANES), F32)],
        compiler_params=_cparams(("parallel", "arbitrary")),
        name="rwkv_scan",
    )(*args)


def _post_kernel(x_ref, yf_ref, yb_ref, bs_ref, vr_ref, gr_ref, oa_ref, ga_ref, gg_ref, mod_ref,
                 lnw_ref, lnb_ref, wa_ref, wr_ref, wo_ref, n2_ref, wi_ref, wo2_ref, o_ref, *, n_chunks, n_sub):
    d = x_ref.shape[2]
    rows = x_ref.shape[1] // n_sub
    d_ff = wo2_ref.shape[0]
    tf = d_ff // n_chunks
    subs = [dict(rs=slice(i * rows, (i + 1) * rows)) for i in range(n_sub)]

    for sb in subs:
        rs = sb['rs']
        y = yf_ref[0, rs, :] + yb_ref[0, rs, :]
        mu = _half_block_sum(y) * (1.0 / HD_R)
        yc = y - mu
        var = _half_block_sum(yc * yc) * (1.0 / HD_R)
        yn = yc * lax.rsqrt(var + EPS_GN) * lnw_ref[...] + lnb_ref[...]
        sb['o_r'] = (yn + bs_ref[0, rs, :] * vr_ref[0, rs, :]) * _sigmoid(gr_ref[0, rs, :])
    for sb in subs:
        rs = sb['rs']
        sb['merged'] = (ga_ref[0, rs, :] * _dot(oa_ref[0, rs, :], wa_ref[...])
                        + gg_ref[0, rs, :] * _dot(sb['o_r'], wr_ref[...]))
    for sb in subs:
        x1 = x_ref[0, sb['rs'], :] + mod_ref[0, :, 2 * d:3 * d] * _dot(sb['merged'], wo_ref[...])
        ms = jnp.mean(x1 * x1, axis=-1, keepdims=True)
        sb['x1'] = x1
        sb['h'] = (x1 * lax.rsqrt(ms + EPS_RMS) * n2_ref[...] * (1.0 + mod_ref[0, :, 4 * d:5 * d])
                   + mod_ref[0, :, 3 * d:4 * d]).astype(BF16)

    def up(sb, c):
        u = jnp.dot(sb['h'], wi_ref[:, c * tf:(c + 1) * tf], preferred_element_type=F32)
        g = jnp.dot(sb['h'], wi_ref[:, d_ff + c * tf:d_ff + (c + 1) * tf], preferred_element_type=F32)
        return u, g

    for sb in subs:
        sb['nxt'] = up(sb, 0)
        sb['acc'] = None
    for c in range(n_chunks):
        for sb in subs:
            u, g = sb['nxt']
            if c + 1 < n_chunks:
                sb['nxt'] = up(sb, c + 1)
            a = ((u * _sigmoid(u)) * g).astype(BF16)
            part = jnp.dot(a, wo2_ref[c * tf:(c + 1) * tf, :], preferred_element_type=F32)
            sb['acc'] = part if sb['acc'] is None else sb['acc'] + part
    for sb in subs:
        o_ref[0, sb['rs'], :] = sb['x1'] + mod_ref[0, :, 5 * d:6 * d] * sb['acc']


def _post(x, zr, gates, yf, yb, bsum, oa, mod, ln_x_w, ln_x_b, wa_bf, wr_bf, wo_bf, norm2_w, w_in_bf, w_out_bf,
          *, tm, n_chunks, n_sub):
    nb, t_len, d = x.shape
    tok = lambda w, col: pl.BlockSpec((1, tm, w), lambda b, i: (b, i, col))
    resident = lambda a: pl.BlockSpec(a.shape, lambda b, i: (0,) * a.ndim, pipeline_mode=pl.Buffered(1))
    lnw, lnb, n2 = ln_x_w.reshape(1, W_R), ln_x_b.reshape(1, W_R), norm2_w.reshape(1, d)
    return pl.pallas_call(
        functools.partial(_post_kernel, n_chunks=n_chunks, n_sub=n_sub),
        grid=(nb, t_len // tm),
        in_specs=[tok(d, 0), tok(W_R, 0), tok(W_R, 0), tok(W_R, 0), tok(COL, ZR_VR), tok(COL, ZR_GR),
                  tok(W_R, 0), tok(d, 0), tok(d, 1),
                  pl.BlockSpec((1, 1, mod.shape[2]), lambda b, i: (b, 0, 0)),
                  resident(lnw), resident(lnb), resident(wa_bf), resident(wr_bf), resident(wo_bf), resident(n2),
                  resident(w_in_bf), resident(w_out_bf)],
        out_specs=tok(d, 0),
        out_shape=jax.ShapeDtypeStruct((nb, t_len, d), F32),
        compiler_params=pltpu.CompilerParams(dimension_semantics=("parallel", "parallel"),
                                             vmem_limit_bytes=INPROJ_VMEM_LIMIT),
        name="post",
    )(x, yf, yb, bsum, zr, zr, oa, gates, gates, mod, lnw, lnb, wa_bf, wr_bf, wo_bf, n2, w_in_bf, w_out_bf)


def _state_to_pairs(s):
    nb = s.shape[0]
    st = jnp.swapaxes(s, -1, -2).reshape(nb, 2, N_HEADS_R // 2, 2, HD_R, HD_R)
    zero = jnp.zeros_like(st[:, :, :, 0])
    top = jnp.concatenate([st[:, :, :, 0], zero], axis=-1)
    bot = jnp.concatenate([zero, st[:, :, :, 1]], axis=-1)
    return jnp.concatenate([top, bot], axis=-2)


def _pairs_to_state(sp):
    nb = sp.shape[0]
    h0 = sp[:, :, :, 0:HD_R, 0:HD_R]
    h1 = sp[:, :, :, HD_R:, HD_R:]
    st = jnp.stack([h0, h1], axis=3).reshape(nb, 2, N_HEADS_R, HD_R, HD_R)
    return jnp.swapaxes(st, -1, -2)


def _layer(x_tok, nb_seq, mod, lp, lam_init, cache, s0_pairs, rope, *, tm, tq, nbb):
    nbm, tmod, d = x_tok.shape
    t_seq = nbm * tmod // nb_seq
    outs = _inproj(x_tok, mod, lp['norm1_w'], lp['w_in'], lp['qkw'], rope, tm=tm)
    qkv, zr, gates = outs[:3]
    kv = outs[3:5] if cache is None else None
    seq = lambda a: a.reshape(a.shape[:-3] + (nb_seq, t_seq, a.shape[-1]))
    oa = _attention(seq(qkv), cache, lp['lam'], lp['subln_w'], lam_init, tq=tq,
                    heads_per_step=1 if cache is not None else N_HEADS_A)
    kk, lw, bb, kd, bsum = _rwkv_prep(zr, lp['w_lora_up'], lp['a_lora_up'], lp['w0'], lp['a0'],
                                      lp['k_k'], lp['k_a'], lp['r_k'], tm=512)
    yf, yb, s_fin = _rwkv_scan(seq(zr), ZR_RR, ZR_VR, seq(kk), seq(lw), seq(bb), seq(kd), s0_pairs, nbb=nbb)
    tokv = lambda a: a.reshape(nbm, tmod, a.shape[-1])
    y = _post(x_tok, zr, gates, tokv(yf), tokv(yb), bsum, tokv(oa), mod, lp['ln_x_w'], lp['ln_x_b'],
              lp['w_attn_br'], lp['w_rwkv_br'], lp['w_out'], lp['norm2_w'], lp['w_ffn_in'], lp['w_ffn_out'],
              tm=512, n_chunks=4, n_sub=2)
    return y, kv, s_fin


def kernel(x_prompt, x_sample, cache_k, cache_v, state_rwkv, c, c_ctx, ada_w, ada_b, norm1_w, norm2_w, w_in, q_norm_w, k_norm_w, lambda_q1, lambda_k1, lambda_q2, lambda_k2, subln_w, w_lora_up, w0, a_lora_up, a0, k_k, k_a, r_k, ln_x_w, ln_x_b, w_attn_br, w_rwkv_br, w_out, w_ffn_in, w_ffn_out):
    depth = ada_w.shape[0]
    batch, seq_len, d = x_prompt.shape
    dec_batch, dec_seq, _ = x_sample.shape
    past = cache_k.shape[2]
    qk_cols = N_HEADS_A * 2 * HD_A
    assert qk_cols == COL and W_R == COL and N_HEADS_A * VD_A == COL

    rope = _rope_tables(dec_seq)
    y_prompt = x_prompt.reshape(1, batch * seq_len, d)
    y_sample = x_sample
    ks_out, vs_out, ss_out = [], [], []
    cond_rows = 16
    for li in range(depth):
        cond = jnp.concatenate([c_ctx[None, :], c, jnp.zeros((cond_rows - 1 - dec_batch, d), F32)], axis=0)
        m = _modulation(cond, ada_w[li], ada_b[li])
        mod_ctx = m[0:1].reshape(1, 1, 6 * d)
        mod_lat = m[1:1 + dec_batch].reshape(dec_batch, 1, 6 * d)

        wi = w_in[li]
        o_lora = 3 * COL + 4 * W_R
        o_gate = o_lora + 4 * LORA
        w_re = jnp.concatenate([wi[:, :o_lora], wi[:, o_lora:o_gate], jnp.zeros((d, COL - 4 * LORA), F32),
                                wi[:, o_gate:]], axis=1).astype(BF16)
        assert w_re.shape[1] == N_TILES * COL
        lp = {
            'norm1_w': norm1_w[li], 'norm2_w': norm2_w[li], 'w_in': w_re,
            'qkw': jnp.stack([jnp.tile(q_norm_w[li], 2 * N_HEADS_A), jnp.tile(k_norm_w[li], 2 * N_HEADS_A)]),
            'lam': jnp.stack([lambda_q1[li], lambda_k1[li], lambda_q2[li], lambda_k2[li]]),
            'subln_w': subln_w[li], 'w_lora_up': w_lora_up[li], 'w0': w0[li], 'a_lora_up': a_lora_up[li],
            'a0': a0[li], 'k_k': k_k[li], 'k_a': k_a[li], 'r_k': r_k[li], 'ln_x_w': ln_x_w[li],
            'ln_x_b': ln_x_b[li], 'w_attn_br': w_attn_br[li].astype(BF16), 'w_rwkv_br': w_rwkv_br[li].astype(BF16),
            'w_out': w_out[li].astype(BF16), 'w_ffn_in': w_ffn_in[li].astype(BF16),
            'w_ffn_out': w_ffn_out[li].astype(BF16),
        }
        lam_init = 0.8 - 0.6 * math.exp(-0.3 * li)

        y_prompt, kv_ctx, s_ctx = _layer(y_prompt, batch, mod_ctx, lp, lam_init, None, None, None,
                                         tm=512, tq=seq_len, nbb=2)
        ks_out.append(kv_ctx[0].reshape(batch, seq_len, N_HEADS_A, 2, HD_A))
        vs_out.append(kv_ctx[1].reshape(batch, seq_len, N_HEADS_A, VD_A))
        ss_out.append(_pairs_to_state(s_ctx))

        cache = (cache_k[:, li].reshape(dec_batch, past, COL).astype(BF16),
                 cache_v[:, li].reshape(dec_batch, past, COL).astype(BF16))
        s0_lat = _state_to_pairs(state_rwkv[:, li])
        y_sample, _, _ = _layer(y_sample, dec_batch, mod_lat, lp, lam_init, cache, s0_lat, rope,
                                tm=512, tq=512, nbb=4)
    new_k = jnp.stack(ks_out, axis=1)
    new_v = jnp.stack(vs_out, axis=1)
    new_state = jnp.stack(ss_out, axis=1)
    return (y_prompt.reshape(batch, seq_len, d), y_sample, new_k, new_v, new_state)
```

```python
import functools
import math

import jax
import jax.numpy as jnp
from jax import lax
from jax.experimental import pallas as pl
from jax.experimental.pallas import tpu as pltpu

F32 = jnp.float32
BF16 = jnp.bfloat16

N_HEADS_A = 4
HD_A = 64
VD_A = 2 * HD_A
N_HEADS_R = 8
HD_R = 64
W_R = N_HEADS_R * HD_R
LORA = 64
GRID_W = 64
ROPE_THETA = 10000.0
EPS_RMS = 1e-6
EPS_GN = 64e-5

LANES = 128
VMEM_BYTES_V7X = 64 * 1024 * 1024
VMEM_LIMIT = 48 * 1024 * 1024
RESIDENT_VMEM_LIMIT = VMEM_BYTES_V7X - 6 * 1024 * 1024

CHUNK = 64
SUB = 8
COL = 512
KV_CHUNK = 512


def _tile_plan(seq_len):
    short = seq_len <= KV_CHUNK
    return dict(
        tm=512,
        tq=min(seq_len, 512),
        heads_per_step=N_HEADS_A if short else 1,
        nbb=2 if short else 4,
        n_chunks=4,
        n_sub=2,
    )

NN = (((1,), (0,)), ((), ()))
NT = (((1,), (1,)), ((), ()))
TN = (((0,), (0,)), ((), ()))


def _dot(a, b, dims=NN):
    return lax.dot_general(a.astype(BF16), b.astype(BF16), dims, preferred_element_type=F32)


def _split2(x):
    hi = x.astype(BF16)
    lo = (x - hi.astype(F32)).astype(BF16)
    return hi, lo


def _split3(x):
    hi = x.astype(BF16)
    r1 = x - hi.astype(F32)
    mid = r1.astype(BF16)
    lo = (r1 - mid.astype(F32)).astype(BF16)
    return hi, mid, lo


def _dot_x3(a, b, dims=NN):
    ah, al = _split2(a)
    bh, bl = _split2(b)
    d = functools.partial(lax.dot_general, dimension_numbers=dims, preferred_element_type=F32)
    return d(ah, bh) + d(ah, bl) + d(al, bh)


def _sigmoid(x):
    return 1.0 / (1.0 + jnp.exp(-x))


def _idiv(x, pow2):
    assert pow2 & (pow2 - 1) == 0
    return x >> int(math.log2(pow2))


def _group_ones(n, group):
    r = _idiv(lax.broadcasted_iota(jnp.int32, (n, n), 0), group)
    c = _idiv(lax.broadcasted_iota(jnp.int32, (n, n), 1), group)
    return jnp.where(r == c, 1.0, 0.0).astype(BF16)


def _group_sum(x, ones_bd):
    ones2 = jnp.concatenate([ones_bd, ones_bd], axis=0)
    outs = []
    for cb in range(x.shape[1] // LANES):
        hi, lo = _split2(x[:, cb * LANES:(cb + 1) * LANES])
        outs.append(jnp.dot(jnp.concatenate([hi, lo], axis=1), ones2, preferred_element_type=F32))
    return outs[0] if len(outs) == 1 else jnp.concatenate(outs, axis=1)


def _half_block_sum(x):
    half = LANES // 2
    outs = []
    for cb in range(x.shape[1] // LANES):
        xb = x[:, cb * LANES:(cb + 1) * LANES]
        low = lax.broadcasted_iota(jnp.int32, xb.shape, 1) < half
        s_lo = jnp.sum(jnp.where(low, xb, 0.0), axis=-1, keepdims=True)
        s_hi = jnp.sum(jnp.where(low, 0.0, xb), axis=-1, keepdims=True)
        outs.append(jnp.where(low, s_lo, s_hi))
    return outs[0] if len(outs) == 1 else jnp.concatenate(outs, axis=1)


def _cparams(sem):
    return pltpu.CompilerParams(dimension_semantics=sem, vmem_limit_bytes=VMEM_LIMIT)


def _mod_kernel(c_ref, w_ref, b_ref, o_ref):
    c = c_ref[...]
    s = c * _sigmoid(c)
    o_ref[...] = _dot_x3(s, w_ref[...]) + b_ref[...]


def _modulation(cond, ada_w, ada_b):
    rows, d = cond.shape
    n = ada_w.shape[1]
    tn = 1536
    return pl.pallas_call(
        _mod_kernel,
        grid=(n // tn,),
        in_specs=[pl.BlockSpec((rows, d), lambda j: (0, 0)),
                  pl.BlockSpec((d, tn), lambda j: (0, j)),
                  pl.BlockSpec((1, tn), lambda j: (0, j))],
        out_specs=pl.BlockSpec((rows, tn), lambda j: (0, j)),
        out_shape=jax.ShapeDtypeStruct((rows, n), F32),
        compiler_params=_cparams(("parallel",)),
        name="modulation",
    )(cond, ada_w, ada_b.reshape(1, n))


def _rope_kernel(cos_ref, sin_ref, *, tm):
    i = pl.program_id(0)
    shift = int(math.log2(GRID_W))
    t = i * tm + lax.broadcasted_iota(jnp.int32, (tm, LANES), 0)
    lane = lax.broadcasted_iota(jnp.int32, (tm, LANES), 1)
    l64 = lane & (HD_A - 1)
    nf = HD_A // 4
    f = (l64 & (nf - 1)).astype(F32)
    inv = jnp.exp(f * (-math.log(ROPE_THETA) / nf))
    pos = jnp.where(l64 < HD_A // 2, t >> shift, t & (GRID_W - 1)).astype(F32)
    ang = pos * inv
    sn = jnp.sin(ang)
    cos_ref[...] = jnp.cos(ang)
    sin_ref[...] = jnp.where((l64 & (2 * nf - 1)) < nf, -sn, sn)


def _rope_tables(t_len):
    assert GRID_W & (GRID_W - 1) == 0
    tm = 512
    return pl.pallas_call(
        functools.partial(_rope_kernel, tm=tm),
        grid=(t_len // tm,),
        out_specs=[pl.BlockSpec((tm, LANES), lambda i: (i, 0))] * 2,
        out_shape=[jax.ShapeDtypeStruct((t_len, LANES), F32)] * 2,
        compiler_params=_cparams(("parallel",)),
        name="rope_tables",
    )()


TILE_Q, TILE_K, TILE_V, TILE_RR, TILE_KR, TILE_VR, TILE_GR, TILE_LORA, TILE_GATE = 0, 1, 2, 3, 4, 5, 6, 7, 8
N_TILES = 12
ZR_RR, ZR_KR, ZR_VR, ZR_GR, ZR_LORA = 0, 1, 2, 3, 4
N_ZR = 5
N_GATE = N_TILES - TILE_GATE
Q_SCALE = HD_A ** -0.5 * math.log2(math.e)


def _rope_apply(y, cos, sin):
    n = y.shape[1]
    lane = lax.broadcasted_iota(jnp.int32, y.shape, 1)
    quarter = HD_A // 4
    first = (lane & (2 * quarter - 1)) < quarter
    swapped = jnp.where(first, pltpu.roll(y, n - quarter, 1), pltpu.roll(y, quarter, 1))
    reps = n // LANES
    cos_f = jnp.concatenate([cos] * reps, axis=1)
    sin_f = jnp.concatenate([sin] * reps, axis=1)
    return y * cos_f + swapped * sin_f


def _inproj_kernel(*refs, latent):
    if latent:
        x_ref, mod_ref, nw_ref, w_ref, qkw_ref, cos_ref, sin_ref, qkv_ref, zr_ref, gate_ref = refs
    else:
        x_ref, mod_ref, nw_ref, w_ref, qkw_ref, qkv_ref, zr_ref, gate_ref, k32_ref, v32_ref = refs
    d = x_ref.shape[2]
    x = x_ref[0]
    ms = jnp.mean(x * x, axis=-1, keepdims=True)
    y = x * lax.rsqrt(ms + EPS_RMS) * nw_ref[...]
    h = (y * (1.0 + mod_ref[0, :, d:2 * d]) + mod_ref[0, :, 0:d]).astype(BF16)

    def tile(j):
        return jnp.dot(h, w_ref[:, j * COL:(j + 1) * COL], preferred_element_type=F32)

    def qk_norm(zz, w_row):
        gms = _half_block_sum(zz * zz) * (1.0 / HD_A)
        return zz * lax.rsqrt(gms + EPS_RMS) * w_row

    def maybe_rope(v):
        return _rope_apply(v, cos_ref[...], sin_ref[...]) if latent else v

    def col(j):
        return slice(j * COL, (j + 1) * COL)

    order = [TILE_Q, TILE_K, TILE_V] + list(range(TILE_GATE, N_TILES)) + list(range(TILE_RR, TILE_GATE))
    z_next = tile(order[0])
    for pos, j in enumerate(order):
        z = z_next
        if pos + 1 < N_TILES:
            z_next = tile(order[pos + 1])
        if j == TILE_Q:
            qkv_ref[0, :, col(0)] = (maybe_rope(qk_norm(z, qkw_ref[0:1, :])) * Q_SCALE).astype(BF16)
        elif j == TILE_K:
            kn = qk_norm(z, qkw_ref[1:2, :])
            qkv_ref[0, :, col(1)] = maybe_rope(kn).astype(BF16)
            if not latent:
                k32_ref[0] = kn
        elif j == TILE_V:
            qkv_ref[0, :, col(2)] = z.astype(BF16)
            if not latent:
                v32_ref[0] = z
        elif j < TILE_GATE:
            zr_ref[0, :, col(j - TILE_RR)] = z
        else:
            gate_ref[0, :, col(j - TILE_GATE)] = _sigmoid(z)


def _inproj(x, mod, norm1_w, w_bf, qkw, rope, *, tm):
    nb, t_len, d = x.shape
    latent = rope is not None
    ncols = w_bf.shape[1]
    in_specs = [pl.BlockSpec((1, tm, d), lambda b, i: (b, i, 0)),
                pl.BlockSpec((1, 1, mod.shape[2]), lambda b, i: (b, 0, 0)),
                pl.BlockSpec((1, d), lambda b, i: (0, 0)),
                pl.BlockSpec((d, ncols), lambda b, i: (0, 0), pipeline_mode=pl.Buffered(1)),
                pl.BlockSpec((2, COL), lambda b, i: (0, 0))]
    args = [x, mod, norm1_w.reshape(1, d), w_bf, qkw]
    if latent:
        in_specs += [pl.BlockSpec((tm, LANES), lambda b, i: (i, 0))] * 2
        args += list(rope)
    widths = [3, N_ZR, N_GATE] + ([] if latent else [1, 1])
    dtypes = [BF16, F32, F32] + ([] if latent else [F32, F32])
    out_specs = [pl.BlockSpec((1, tm, n * COL), lambda b, i: (b, i, 0)) for n in widths]
    out_shape = [jax.ShapeDtypeStruct((nb, t_len, n * COL), dt) for n, dt in zip(widths, dtypes)]
    return pl.pallas_call(
        functools.partial(_inproj_kernel, latent=latent),
        grid=(nb, t_len // tm),
        in_specs=in_specs,
        out_specs=out_specs,
        out_shape=out_shape,
        compiler_params=pltpu.CompilerParams(dimension_semantics=("parallel", "parallel"),
                                             vmem_limit_bytes=RESIDENT_VMEM_LIMIT),
        name="inproj_latent" if latent else "inproj_context",
    )(*args)


def _attn_kernel(*refs, has_cache, lam_init):
    if has_cache:
        q_ref, k_ref, v_ref, ck_ref, cv_ref, lam_ref, sw_ref, o_ref = refs
    else:
        q_ref, k_ref, v_ref, lam_ref, sw_ref, o_ref = refs
    tq = q_ref.shape[1]
    t_k = k_ref.shape[1]
    lp = lam_ref[...]
    lam = (jnp.exp(jnp.sum(lp[0:1] * lp[1:2], axis=-1, keepdims=True))
           - jnp.exp(jnp.sum(lp[2:3] * lp[3:4], axis=-1, keepdims=True)) + lam_init)

    chunks = [(ck_ref, cv_ref, slice(None))] if has_cache else []
    kc = min(KV_CHUNK, t_k)
    chunks += [(k_ref, v_ref, slice(c * kc, (c + 1) * kc)) for c in range(t_k // kc)]

    heads = []
    for h in range(q_ref.shape[2] // LANES):
        hs = slice(h * LANES, (h + 1) * LANES)
        q = q_ref[0, :, hs].astype(F32)
        lane = lax.broadcasted_iota(jnp.int32, q.shape, 1)
        qs = jnp.concatenate([jnp.where(lane < HD_A, q, 0.0), jnp.where(lane >= HD_A, q, 0.0)],
                             axis=0).astype(BF16)
        heads.append(dict(hs=hs, qs=qs, m=None, acc=None))

    def scores(hd, c):
        kr, _, rs = chunks[c]
        return lax.dot_general(hd['qs'], kr[0, rs, hd['hs']], NT, preferred_element_type=F32)

    for hd in heads:
        hd['s_next'] = scores(hd, 0)
    for c, (_, vr, rs) in enumerate(chunks):
        for hd in heads:
            s = hd['s_next']
            if c + 1 < len(chunks):
                hd['s_next'] = scores(hd, c + 1)
            mx = jnp.max(s, axis=-1, keepdims=True)
            m, acc = hd['m'], hd['acc']
            m_new = mx if m is None else jnp.maximum(m, mx)
            p = jnp.exp2(s - m_new)
            v_c = vr[0, rs, hd['hs']]
            v_aug = jnp.concatenate([v_c, jnp.ones_like(v_c)], axis=1)
            pv = jnp.dot(p.astype(BF16), v_aug, preferred_element_type=F32)
            hd['acc'] = pv if m is None else jnp.exp2(m - m_new) * acc + pv
            hd['m'] = m_new
    for hd in heads:
        acc = hd['acc']
        o = acc[:, 0:VD_A] / acc[:, VD_A:2 * VD_A]
        o = o[0:tq] - lam * o[tq:2 * tq]
        ms = jnp.mean(o * o, axis=-1, keepdims=True)
        o_ref[0, :, hd['hs']] = (o * lax.rsqrt(ms + EPS_RMS) * sw_ref[...] * (1.0 - lam_init)).astype(o_ref.dtype)


def _attention(qkv, cache, lam_p, subln_w, lam_init, *, tq, heads_per_step):
    nb, t_len, _ = qkv.shape
    hw = heads_per_step * LANES
    n_hg = N_HEADS_A // heads_per_step
    in_specs = [pl.BlockSpec((1, tq, hw), lambda b, h, i: (b, i, h)),
                pl.BlockSpec((1, t_len, hw), lambda b, h, i: (b, 0, n_hg + h)),
                pl.BlockSpec((1, t_len, hw), lambda b, h, i: (b, 0, 2 * n_hg + h))]
    args = [qkv, qkv, qkv]
    if cache is not None:
        ck, cv = cache
        past = ck.shape[1]
        in_specs += [pl.BlockSpec((1, past, hw), lambda b, h, i: (b, 0, h))] * 2
        args += [ck, cv]
    in_specs += [pl.BlockSpec((4, HD_A), lambda b, h, i: (0, 0)),
                 pl.BlockSpec((1, VD_A), lambda b, h, i: (0, 0))]
    args += [lam_p, subln_w.reshape(1, VD_A)]
    return pl.pallas_call(
        functools.partial(_attn_kernel, has_cache=cache is not None, lam_init=lam_init),
        grid=(nb, n_hg, t_len // tq),
        in_specs=in_specs,
        out_specs=pl.BlockSpec((1, tq, hw), lambda b, h, i: (b, i, h)),
        out_shape=jax.ShapeDtypeStruct((nb, t_len, N_HEADS_A * VD_A), BF16),
        compiler_params=_cparams(("parallel", "parallel", "arbitrary")),
        name="attention_latent" if cache is not None else "attention_context",
    )(*args)


def _prep_kernel(r_ref, k_ref, la_ref, wup_ref, aup_ref, w0_ref, a0_ref, kk_ref_p, ka_ref, rk_ref,
                 kk_out, lw_out, a_out, bs_out):
    r = r_ref[0]
    k = k_ref[0]
    la = la_ref[0]
    ones_bd = _group_ones(LANES, HD_R)
    kraw = k * kk_ref_p[...]
    nrm = jnp.sqrt(_group_sum(kraw * kraw, ones_bd))
    kk = kraw / jnp.maximum(nrm, 1e-12)
    kk_out[0] = kk
    wl = jnp.tanh(la[:, 0:2 * LORA])
    al = la[:, 2 * LORA:4 * LORA]
    lane = lax.broadcasted_iota(jnp.int32, wl.shape, 1)
    rkd = None
    for z in range(2):
        sel = _idiv(lane, LORA) == z
        w = w0_ref[z:z + 1, :] + _dot_x3(jnp.where(sel, wl, 0.0), wup_ref[...])
        lw_out[z, 0] = -math.exp(-0.5) * _sigmoid(w)
        a = _sigmoid(a0_ref[z:z + 1, :] + _dot_x3(jnp.where(sel, al, 0.0), aup_ref[...]))
        a_out[z, 0] = a
        kd = k * (1.0 + (a - 1.0) * ka_ref[...])
        rkd = r * kd if rkd is None else rkd + r * kd
    bs_out[0] = _group_sum(rkd * rk_ref[...], ones_bd)


def _rwkv_prep(z, w_lora_up, a_lora_up, w0, a0, k_k, k_a, r_k, *, tm):
    nb, t_len, _ = z.shape
    tok = lambda col: pl.BlockSpec((1, tm, COL), lambda b, i: (b, i, col))
    par = lambda rows: pl.BlockSpec((rows, W_R), lambda b, i: (0, 0))
    dir_spec = pl.BlockSpec((2, 1, tm, W_R), lambda b, i: (0, b, i, 0))
    tok_out = pl.BlockSpec((1, tm, W_R), lambda b, i: (b, i, 0))
    one = jax.ShapeDtypeStruct((nb, t_len, W_R), F32)
    two = jax.ShapeDtypeStruct((2, nb, t_len, W_R), F32)
    return pl.pallas_call(
        _prep_kernel,
        grid=(nb, t_len // tm),
        in_specs=[tok(ZR_RR), tok(ZR_KR),
                  pl.BlockSpec((1, tm, 4 * LORA), lambda b, i: (b, i, ZR_LORA * COL // (4 * LORA))),
                  par(2 * LORA), par(2 * LORA), par(2), par(2), par(1), par(1), par(1)],
        out_specs=[tok_out, dir_spec, dir_spec, tok_out],
        out_shape=[one, two, two, one],
        compiler_params=_cparams(("parallel", "parallel")),
        name="rwkv_prep",
    )(z, z, z, w_lora_up.reshape(2 * LORA, W_R), a_lora_up.reshape(2 * LORA, W_R), w0, a0,
      k_k.reshape(1, W_R), k_a.reshape(1, W_R), r_k.reshape(1, W_R))


def _pair_rows(y):
    lane = lax.broadcasted_iota(jnp.int32, y.shape, 1) & (LANES - 1)
    return jnp.concatenate([jnp.where(lane < HD_R, y, 0.0).astype(BF16),
                            jnp.where(lane >= HD_R, y, 0.0).astype(BF16)], axis=0)


def _rwkv_chunk_kernel(rf, kf, kkf, vf, lwf, af, rb, kb, kkb, vb, lwb, ab, ka_ref, *rest, nbb, zero_init):
    if zero_init:
        yf_ref, yb_ref, sfin_ref, s_scr = rest
    else:
        s0_ref, yf_ref, yb_ref, sfin_ref, s_scr = rest
    c = pl.program_id(1)
    n_pairs = W_R // LANES

    @pl.when(c == 0)
    def _():
        s_scr[...] = jnp.zeros_like(s_scr) if zero_init else s0_ref[...]

    C = CHUNK
    t_i = lax.broadcasted_iota(jnp.int32, (C, C), 0)
    i_i = lax.broadcasted_iota(jnp.int32, (C, C), 1)
    t2 = lax.broadcasted_iota(jnp.int32, (C, LANES), 0)
    i2 = lax.broadcasted_iota(jnp.int32, (C, LANES), 1) & (HD_R - 1)
    same_sub = _idiv(t2, SUB) == _idiv(i2, SUB)
    eye = lax.broadcasted_iota(jnp.int32, (LANES, LANES), 0) == lax.broadcasted_iota(jnp.int32, (LANES, LANES), 1)
    same_head = (_idiv(lax.broadcasted_iota(jnp.int32, (LANES, LANES), 0), HD_R)
                 == _idiv(lax.broadcasted_iota(jnp.int32, (LANES, LANES), 1), HD_R))
    masks = {}
    for rev in (False, True):
        tri = (i_i >= t_i) if rev else (i_i <= t_i)
        masks[rev] = dict(tri=jnp.where(tri, 1.0, 0.0).astype(BF16),
                          strict=(i2 > t2) if rev else (i2 < t2),
                          incl=(i2 >= t2) if rev else (i2 <= t2))

    chains = []
    for bi in range(nbb):
        for rev, (r_, k_, kk_, v_, lw_, a_) in ((False, (rf, kf, kkf, vf, lwf, af)),
                                                 (True, (rb, kb, kkb, vb, lwb, ab))):
            for p in range(n_pairs):
                sl = slice(p * LANES, (p + 1) * LANES)
                a = a_[bi, :, sl]
                kk = kk_[bi, :, sl]
                chains.append(dict(bi=bi, rev=rev, p=p, r=r_[bi, :, sl], kk=kk, v=v_[bi, :, sl], lw=lw_[bi, :, sl],
                                   b=kk * a, kd=k_[bi, :, sl] * (1.0 + (a - 1.0) * ka_ref[:, sl])))

    for ch in chains:
        h, m, l = _split3(ch['lw'])
        cl = jnp.dot(masks[ch['rev']]['tri'], jnp.concatenate([h, m, l], axis=1), preferred_element_type=F32)
        ch['cl'] = cl[:, 0:LANES] + cl[:, LANES:2 * LANES] + cl[:, 2 * LANES:3 * LANES]
    for ch in chains:
        cl, lw = ch['cl'], ch['lw']
        p_in = jnp.exp(cl)
        p_inv = jnp.exp(-cl)
        p_ex = jnp.exp(cl - lw)
        ch['pc'] = jnp.exp(jnp.sum(lw, axis=0, keepdims=True))
        p_end = ch['pc'] * p_inv
        ch['at'] = -ch['kk'] * p_ex
        ch['rt'] = ch['r'] * p_in
        ch['bt'] = ch['b'] * p_inv
        ch['kt'] = ch['kd'] * p_inv
        ch['bh'] = ch['b'] * p_end
        ch['kh'] = ch['kd'] * p_end
        ch['S'] = s_scr[ch['bi'], int(ch['rev']), ch['p']]
    for ch in chains:
        g = lax.dot_general(jnp.concatenate([ch['at'], ch['rt']], axis=0).astype(BF16),
                            jnp.concatenate([_pair_rows(ch['bt']), _pair_rows(ch['kt'])], axis=0),
                            NT, preferred_element_type=F32)
        mk = masks[ch['rev']]
        ch['lab'] = jnp.where(mk['strict'], g[0:C, 0:LANES], 0.0)
        ch['lak'] = jnp.where(mk['strict'], g[0:C, LANES:2 * LANES], 0.0)
        ch['lrb'] = jnp.where(mk['incl'], g[C:2 * C, 0:LANES], 0.0)
        ch['lrk'] = jnp.where(mk['incl'], g[C:2 * C, LANES:2 * LANES], 0.0)
        ch['vbd'] = _pair_rows(ch['v'])
    for ch in chains:
        x0 = jnp.dot(jnp.concatenate([ch['at'], ch['lak']], axis=1).astype(BF16),
                     jnp.concatenate([ch['S'].astype(BF16), ch['vbd']], axis=0), preferred_element_type=F32)
        nd = jnp.where(same_sub, ch['lab'], 0.0)
        no = jnp.where(same_sub, 0.0, ch['lab'])
        ch['zc'] = jnp.concatenate([x0, no], axis=1)
        ch['nj'] = nd
    n1 = int(math.log2(SUB))
    for lev in range(n1):
        for ch in chains:
            if lev < n1 - 1:
                rr = jnp.dot(ch['nj'].astype(BF16), _pair_rows(jnp.concatenate([ch['zc'], ch['nj']], axis=1)),
                             preferred_element_type=F32)
                ch['zc'] = ch['zc'] + rr[:, 0:2 * LANES]
                ch['nj'] = rr[:, 2 * LANES:3 * LANES]
            else:
                ch['zc'] = ch['zc'] + jnp.dot(ch['nj'].astype(BF16), _pair_rows(ch['zc']),
                                              preferred_element_type=F32)
    for ch in chains:
        ch['x'] = ch['zc'][:, 0:LANES]
        ch['mj'] = ch['zc'][:, LANES:2 * LANES]
    n2 = int(math.log2(C // SUB))
    for lev in range(n2):
        for ch in chains:
            if lev < n2 - 1:
                rr = jnp.dot(ch['mj'].astype(BF16), _pair_rows(jnp.concatenate([ch['x'], ch['mj']], axis=1)),
                             preferred_element_type=F32)
                ch['x'] = ch['x'] + rr[:, 0:LANES]
                ch['mj'] = rr[:, LANES:2 * LANES]
            else:
                ch['x'] = ch['x'] + jnp.dot(ch['mj'].astype(BF16), _pair_rows(ch['x']), preferred_element_type=F32)
    for ch in chains:
        u = ch['x']
        ubd = _pair_rows(u)
        y = jnp.dot(jnp.concatenate([ch['rt'], ch['lrb'], ch['lrk']], axis=1).astype(BF16),
                    jnp.concatenate([ch['S'].astype(BF16), ubd, ch['vbd']], axis=0), preferred_element_type=F32)
        dg = jnp.where(eye, jnp.broadcast_to(ch['pc'], (LANES, LANES)), 0.0)
        s_new = lax.dot_general(jnp.concatenate([ch['bh'], ch['kh'], dg], axis=0).astype(BF16),
                                jnp.concatenate([u, ch['v'], ch['S']], axis=0).astype(BF16),
                                TN, preferred_element_type=F32)
        s_new = jnp.where(same_head, s_new, 0.0)
        sl = slice(ch['p'] * LANES, (ch['p'] + 1) * LANES)
        if ch['rev']:
            yb_ref[ch['bi'], :, sl] = y
        else:
            yf_ref[ch['bi'], :, sl] = y
        s_scr[ch['bi'], int(ch['rev']), ch['p']] = s_new

    @pl.when(c == pl.num_programs(1) - 1)
    def _():
        sfin_ref[...] = s_scr[...]


def _rwkv_scan(r_src, r_col, k_col, v_col, kk, lw, aa, k_a, s0, *, nbb):
    nb, t_len, _ = kk.shape
    nc = t_len // CHUNK
    n_pairs = W_R // LANES

    def tok(col, rev):
        return pl.BlockSpec((nbb, CHUNK, W_R), (lambda g, c: (g, nc - 1 - c, col)) if rev else (lambda g, c: (g, c, col)))

    def dirs(z, rev):
        return pl.BlockSpec((None, nbb, CHUNK, W_R),
                            (lambda g, c: (z, g, nc - 1 - c, 0)) if rev else (lambda g, c: (z, g, c, 0)))

    st_spec = pl.BlockSpec((nbb, 2, n_pairs, LANES, LANES), lambda g, c: (g, 0, 0, 0, 0))
    in_specs, args = [], []
    for rev in (False, True):
        z = int(rev)
        in_specs += [tok(r_col, rev), tok(k_col, rev), tok(0, rev), tok(v_col, rev), dirs(z, rev), dirs(z, rev)]
        args += [r_src, r_src, kk, r_src, lw, aa]
    in_specs.append(pl.BlockSpec((1, W_R), lambda g, c: (0, 0)))
    args.append(k_a.reshape(1, W_R))
    if s0 is not None:
        in_specs.append(st_spec)
        args.append(s0)
    yshape = jax.ShapeDtypeStruct((nb, t_len, W_R), F32)
    return pl.pallas_call(
        functools.partial(_rwkv_chunk_kernel, nbb=nbb, zero_init=s0 is None),
        grid=(nb // nbb, nc),
        in_specs=in_specs,
        out_specs=[tok(0, False), tok(0, True), st_spec],
        out_shape=[yshape, yshape, jax.ShapeDtypeStruct((nb, 2, n_pairs, LANES, LANES), F32)],
        scratch_shapes=[pltpu.VMEM((nbb, 2, n_pairs, LANES, LANES), F32)],
        compiler_params=_cparams(("parallel", "arbitrary")),
        name="rwkv_scan",
    )(*args)


def _post_kernel(x_ref, yf_ref, yb_ref, bs_ref, vr_ref, gr_ref, oa_ref, ga_ref, gg_ref, mod_ref,
                 lnw_ref, lnb_ref, wa_ref, wr_ref, wo_ref, n2_ref, wi_ref, wo2_ref, o_ref, *, n_chunks, n_sub):
    d = x_ref.shape[2]
    rows = x_ref.shape[1] // n_sub
    d_ff = wo2_ref.shape[0]
    tf = d_ff // n_chunks
    subs = [dict(rs=slice(i * rows, (i + 1) * rows)) for i in range(n_sub)]

    for sb in subs:
        rs = sb['rs']
        y = yf_ref[0, rs, :] + yb_ref[0, rs, :]
        mu = _half_block_sum(y) * (1.0 / HD_R)
        yc = y - mu
        var = _half_block_sum(yc * yc) * (1.0 / HD_R)
        yn = yc * lax.rsqrt(var + EPS_GN) * lnw_ref[...] + lnb_ref[...]
        sb['o_r'] = (yn + bs_ref[0, rs, :] * vr_ref[0, rs, :]) * _sigmoid(gr_ref[0, rs, :])
    for sb in subs:
        rs = sb['rs']
        sb['merged'] = (ga_ref[0, rs, :] * _dot(oa_ref[0, rs, :], wa_ref[...])
                        + gg_ref[0, rs, :] * _dot(sb['o_r'], wr_ref[...]))
    for sb in subs:
        x1 = x_ref[0, sb['rs'], :] + mod_ref[0, :, 2 * d:3 * d] * _dot(sb['merged'], wo_ref[...])
        ms = jnp.mean(x1 * x1, axis=-1, keepdims=True)
        sb['x1'] = x1
        sb['h'] = (x1 * lax.rsqrt(ms + EPS_RMS) * n2_ref[...] * (1.0 + mod_ref[0, :, 4 * d:5 * d])
                   + mod_ref[0, :, 3 * d:4 * d]).astype(BF16)

    def up(sb, c):
        u = jnp.dot(sb['h'], wi_ref[:, c * tf:(c + 1) * tf], preferred_element_type=F32)
        g = jnp.dot(sb['h'], wi_ref[:, d_ff + c * tf:d_ff + (c + 1) * tf], preferred_element_type=F32)
        return u, g

    for sb in subs:
        sb['nxt'] = up(sb, 0)
        sb['acc'] = None
    for c in range(n_chunks):
        for sb in subs:
            u, g = sb['nxt']
            if c + 1 < n_chunks:
                sb['nxt'] = up(sb, c + 1)
            a = ((u * _sigmoid(u)) * g).astype(BF16)
            part = jnp.dot(a, wo2_ref[c * tf:(c + 1) * tf, :], preferred_element_type=F32)
            sb['acc'] = part if sb['acc'] is None else sb['acc'] + part
    for sb in subs:
        o_ref[0, sb['rs'], :] = sb['x1'] + mod_ref[0, :, 5 * d:6 * d] * sb['acc']


def _post(x, zr, gates, yf, yb, bsum, oa, mod, ln_x_w, ln_x_b, wa_bf, wr_bf, wo_bf, norm2_w, w_in_bf, w_out_bf,
          *, tm, n_chunks, n_sub):
    nb, t_len, d = x.shape
    tok = lambda w, col: pl.BlockSpec((1, tm, w), lambda b, i: (b, i, col))
    resident = lambda a: pl.BlockSpec(a.shape, lambda b, i: (0,) * a.ndim, pipeline_mode=pl.Buffered(1))
    lnw, lnb, n2 = ln_x_w.reshape(1, W_R), ln_x_b.reshape(1, W_R), norm2_w.reshape(1, d)
    return pl.pallas_call(
        functools.partial(_post_kernel, n_chunks=n_chunks, n_sub=n_sub),
        grid=(nb, t_len // tm),
        in_specs=[tok(d, 0), tok(W_R, 0), tok(W_R, 0), tok(W_R, 0), tok(COL, ZR_VR), tok(COL, ZR_GR),
                  tok(W_R, 0), tok(d, 0), tok(d, 1),
                  pl.BlockSpec((1, 1, mod.shape[2]), lambda b, i: (b, 0, 0)),
                  resident(lnw), resident(lnb), resident(wa_bf), resident(wr_bf), resident(wo_bf), resident(n2),
                  resident(w_in_bf), resident(w_out_bf)],
        out_specs=tok(d, 0),
        out_shape=jax.ShapeDtypeStruct((nb, t_len, d), F32),
        compiler_params=pltpu.CompilerParams(dimension_semantics=("parallel", "parallel"),
                                             vmem_limit_bytes=RESIDENT_VMEM_LIMIT),
        name="post",
    )(x, yf, yb, bsum, zr, zr, oa, gates, gates, mod, lnw, lnb, wa_bf, wr_bf, wo_bf, n2, w_in_bf, w_out_bf)


def _state_to_pairs(s):
    nb = s.shape[0]
    st = jnp.swapaxes(s, -1, -2).reshape(nb, 2, N_HEADS_R // 2, 2, HD_R, HD_R)
    zero = jnp.zeros_like(st[:, :, :, 0])
    top = jnp.concatenate([st[:, :, :, 0], zero], axis=-1)
    bot = jnp.concatenate([zero, st[:, :, :, 1]], axis=-1)
    return jnp.concatenate([top, bot], axis=-2)


def _pairs_to_state(sp):
    nb = sp.shape[0]
    h0 = sp[:, :, :, 0:HD_R, 0:HD_R]
    h1 = sp[:, :, :, HD_R:, HD_R:]
    st = jnp.stack([h0, h1], axis=3).reshape(nb, 2, N_HEADS_R, HD_R, HD_R)
    return jnp.swapaxes(st, -1, -2)


def _layer(x_tok, nb_seq, mod, lp, lam_init, cache, s0_pairs, rope):
    nbm, tmod, d = x_tok.shape
    t_seq = nbm * tmod // nb_seq
    plan = _tile_plan(t_seq)
    outs = _inproj(x_tok, mod, lp['norm1_w'], lp['w_in'], lp['qkw'], rope, tm=plan['tm'])
    qkv, zr, gates = outs[:3]
    kv = outs[3:5] if cache is None else None
    seq = lambda a: a.reshape(a.shape[:-3] + (nb_seq, t_seq, a.shape[-1]))
    oa = _attention(seq(qkv), cache, lp['lam'], lp['subln_w'], lam_init, tq=plan['tq'],
                    heads_per_step=plan['heads_per_step'])
    kk, lw, aa, bsum = _rwkv_prep(zr, lp['w_lora_up'], lp['a_lora_up'], lp['w0'], lp['a0'],
                                  lp['k_k'], lp['k_a'], lp['r_k'], tm=plan['tm'])
    yf, yb, s_fin = _rwkv_scan(seq(zr), ZR_RR, ZR_KR, ZR_VR, seq(kk), seq(lw), seq(aa), lp['k_a'], s0_pairs,
                               nbb=plan['nbb'])
    tokv = lambda a: a.reshape(nbm, tmod, a.shape[-1])
    y = _post(x_tok, zr, gates, tokv(yf), tokv(yb), bsum, tokv(oa), mod, lp['ln_x_w'], lp['ln_x_b'],
              lp['w_attn_br'], lp['w_rwkv_br'], lp['w_out'], lp['norm2_w'], lp['w_ffn_in'], lp['w_ffn_out'],
              tm=plan['tm'], n_chunks=plan['n_chunks'], n_sub=plan['n_sub'])
    return y, kv, s_fin


def kernel(x_prompt, x_sample, cache_k, cache_v, state_rwkv, c, c_ctx, ada_w, ada_b, norm1_w, norm2_w, w_in, q_norm_w, k_norm_w, lambda_q1, lambda_k1, lambda_q2, lambda_k2, subln_w, w_lora_up, w0, a_lora_up, a0, k_k, k_a, r_k, ln_x_w, ln_x_b, w_attn_br, w_rwkv_br, w_out, w_ffn_in, w_ffn_out):
    depth = ada_w.shape[0]
    batch, seq_len, d = x_prompt.shape
    dec_batch, dec_seq, _ = x_sample.shape
    past = cache_k.shape[2]
    qk_cols = N_HEADS_A * 2 * HD_A
    assert qk_cols == COL and W_R == COL and N_HEADS_A * VD_A == COL

    rope = _rope_tables(dec_seq)
    y_prompt = x_prompt.reshape(1, batch * seq_len, d)
    y_sample = x_sample
    ks_out, vs_out, ss_out = [], [], []
    cond_rows = 16
    for li in range(depth):
        cond = jnp.concatenate([c_ctx[None, :], c, jnp.zeros((cond_rows - 1 - dec_batch, d), F32)], axis=0)
        m = _modulation(cond, ada_w[li], ada_b[li])
        mod_ctx = m[0:1].reshape(1, 1, 6 * d)
        mod_lat = m[1:1 + dec_batch].reshape(dec_batch, 1, 6 * d)

        wi = w_in[li]
        o_lora = 3 * COL + 4 * W_R
        o_gate = o_lora + 4 * LORA
        w_re = jnp.concatenate([wi[:, :o_lora], wi[:, o_lora:o_gate], jnp.zeros((d, COL - 4 * LORA), F32),
                                wi[:, o_gate:]], axis=1).astype(BF16)
        assert w_re.shape[1] == N_TILES * COL
        lp = {
            'norm1_w': norm1_w[li], 'norm2_w': norm2_w[li], 'w_in': w_re,
            'qkw': jnp.stack([jnp.tile(q_norm_w[li], 2 * N_HEADS_A), jnp.tile(k_norm_w[li], 2 * N_HEADS_A)]),
            'lam': jnp.stack([lambda_q1[li], lambda_k1[li], lambda_q2[li], lambda_k2[li]]),
            'subln_w': subln_w[li], 'w_lora_up': w_lora_up[li], 'w0': w0[li], 'a_lora_up': a_lora_up[li],
            'a0': a0[li], 'k_k': k_k[li], 'k_a': k_a[li], 'r_k': r_k[li], 'ln_x_w': ln_x_w[li],
            'ln_x_b': ln_x_b[li], 'w_attn_br': w_attn_br[li].astype(BF16), 'w_rwkv_br': w_rwkv_br[li].astype(BF16),
            'w_out': w_out[li].astype(BF16), 'w_ffn_in': w_ffn_in[li].astype(BF16),
            'w_ffn_out': w_ffn_out[li].astype(BF16),
        }
        lam_init = 0.8 - 0.6 * math.exp(-0.3 * li)

        y_prompt, kv_ctx, s_ctx = _layer(y_prompt, batch, mod_ctx, lp, lam_init, None, None, None)
        ks_out.append(kv_ctx[0].reshape(batch, seq_len, N_HEADS_A, 2, HD_A))
        vs_out.append(kv_ctx[1].reshape(batch, seq_len, N_HEADS_A, VD_A))
        ss_out.append(_pairs_to_state(s_ctx))

        cache = (cache_k[:, li].reshape(dec_batch, past, COL).astype(BF16),
                 cache_v[:, li].reshape(dec_batch, past, COL).astype(BF16))
        s0_lat = _state_to_pairs(state_rwkv[:, li])
        y_sample, _, _ = _layer(y_sample, dec_batch, mod_lat, lp, lam_init, cache, s0_lat, rope)
    new_k = jnp.stack(ks_out, axis=1)
    new_v = jnp.stack(vs_out, axis=1)
    new_state = jnp.stack(ss_out, axis=1)
    return (y_prompt.reshape(batch, seq_len, d), y_sample, new_k, new_v, new_state)
```

```python
import functools
import math

import jax
import jax.numpy as jnp
from jax import lax
from jax.experimental import pallas as pl
from jax.experimental.pallas import tpu as pltpu

F32 = jnp.float32
BF16 = jnp.bfloat16

N_HEADS_A = 4
HD_A = 64
VD_A = 2 * HD_A
N_HEADS_R = 8
HD_R = 64
W_R = N_HEADS_R * HD_R
LORA = 64
GRID_W = 64
ROPE_THETA = 10000.0
EPS_RMS = 1e-6
EPS_GN = 64e-5

LANES = 128
VMEM_BYTES_V7X = 64 * 1024 * 1024
VMEM_LIMIT = 48 * 1024 * 1024
RESIDENT_VMEM_LIMIT = VMEM_BYTES_V7X - 6 * 1024 * 1024

CHUNK = 64
SUB = 8
COL = 512
KV_CHUNK = 512


def _tile_plan(seq_len):
    short = seq_len <= KV_CHUNK
    return dict(
        tm=512,
        tq=min(seq_len, 512),
        heads_per_step=N_HEADS_A if short else 1,
        nbb=4,
        n_chunks=4,
        n_sub=2,
    )

NN = (((1,), (0,)), ((), ()))
NT = (((1,), (1,)), ((), ()))
TN = (((0,), (0,)), ((), ()))


def _dot(a, b, dims=NN):
    return lax.dot_general(a.astype(BF16), b.astype(BF16), dims, preferred_element_type=F32)


def _split2(x):
    hi = x.astype(BF16)
    lo = (x - hi.astype(F32)).astype(BF16)
    return hi, lo


def _split3(x):
    hi = x.astype(BF16)
    r1 = x - hi.astype(F32)
    mid = r1.astype(BF16)
    lo = (r1 - mid.astype(F32)).astype(BF16)
    return hi, mid, lo


def _dot_x3(a, b, dims=NN):
    ah, al = _split2(a)
    bh, bl = _split2(b)
    d = functools.partial(lax.dot_general, dimension_numbers=dims, preferred_element_type=F32)
    return d(ah, bh) + d(ah, bl) + d(al, bh)


def _sigmoid(x):
    return 1.0 / (1.0 + jnp.exp(-x))


def _idiv(x, pow2):
    assert pow2 & (pow2 - 1) == 0
    return x >> int(math.log2(pow2))


def _group_ones(n, group):
    r = _idiv(lax.broadcasted_iota(jnp.int32, (n, n), 0), group)
    c = _idiv(lax.broadcasted_iota(jnp.int32, (n, n), 1), group)
    return jnp.where(r == c, 1.0, 0.0).astype(BF16)


def _group_sum(x, ones_bd):
    ones2 = jnp.concatenate([ones_bd, ones_bd], axis=0)
    outs = []
    for cb in range(x.shape[1] // LANES):
        hi, lo = _split2(x[:, cb * LANES:(cb + 1) * LANES])
        outs.append(jnp.dot(jnp.concatenate([hi, lo], axis=1), ones2, preferred_element_type=F32))
    return outs[0] if len(outs) == 1 else jnp.concatenate(outs, axis=1)


def _half_block_sum(x):
    half = LANES // 2
    outs = []
    for cb in range(x.shape[1] // LANES):
        xb = x[:, cb * LANES:(cb + 1) * LANES]
        low = lax.broadcasted_iota(jnp.int32, xb.shape, 1) < half
        s_lo = jnp.sum(jnp.where(low, xb, 0.0), axis=-1, keepdims=True)
        s_hi = jnp.sum(jnp.where(low, 0.0, xb), axis=-1, keepdims=True)
        outs.append(jnp.where(low, s_lo, s_hi))
    return outs[0] if len(outs) == 1 else jnp.concatenate(outs, axis=1)


def _cparams(sem):
    return pltpu.CompilerParams(dimension_semantics=sem, vmem_limit_bytes=VMEM_LIMIT)


def _mod_kernel(c_ref, w_ref, b_ref, o_ref):
    c = c_ref[...]
    s = c * _sigmoid(c)
    o_ref[...] = _dot_x3(s, w_ref[...]) + b_ref[...]


def _modulation(cond, ada_w, ada_b):
    rows, d = cond.shape
    n = ada_w.shape[1]
    tn = 1536
    return pl.pallas_call(
        _mod_kernel,
        grid=(n // tn,),
        in_specs=[pl.BlockSpec((rows, d), lambda j: (0, 0)),
                  pl.BlockSpec((d, tn), lambda j: (0, j)),
                  pl.BlockSpec((1, tn), lambda j: (0, j))],
        out_specs=pl.BlockSpec((rows, tn), lambda j: (0, j)),
        out_shape=jax.ShapeDtypeStruct((rows, n), F32),
        compiler_params=_cparams(("parallel",)),
        name="modulation",
    )(cond, ada_w, ada_b.reshape(1, n))


def _rope_kernel(cos_ref, sin_ref, *, tm):
    i = pl.program_id(0)
    shift = int(math.log2(GRID_W))
    t = i * tm + lax.broadcasted_iota(jnp.int32, (tm, LANES), 0)
    lane = lax.broadcasted_iota(jnp.int32, (tm, LANES), 1)
    l64 = lane & (HD_A - 1)
    nf = HD_A // 4
    f = (l64 & (nf - 1)).astype(F32)
    inv = jnp.exp(f * (-math.log(ROPE_THETA) / nf))
    pos = jnp.where(l64 < HD_A // 2, t >> shift, t & (GRID_W - 1)).astype(F32)
    ang = pos * inv
    sn = jnp.sin(ang)
    cos_ref[...] = jnp.cos(ang)
    sin_ref[...] = jnp.where((l64 & (2 * nf - 1)) < nf, -sn, sn)


def _rope_tables(t_len):
    assert GRID_W & (GRID_W - 1) == 0
    tm = 512
    return pl.pallas_call(
        functools.partial(_rope_kernel, tm=tm),
        grid=(t_len // tm,),
        out_specs=[pl.BlockSpec((tm, LANES), lambda i: (i, 0))] * 2,
        out_shape=[jax.ShapeDtypeStruct((t_len, LANES), F32)] * 2,
        compiler_params=_cparams(("parallel",)),
        name="rope_tables",
    )()


TILE_Q, TILE_K, TILE_V, TILE_RR, TILE_KR, TILE_VR, TILE_GR, TILE_LORA, TILE_GATE = 0, 1, 2, 3, 4, 5, 6, 7, 8
N_TILES = 12
LORA_COLS = 4 * LORA
ZR_RR, ZR_KR, ZR_VR, ZR_GR, ZR_LORA = 0, 1, 2, 3, 4
ZR_COLS = ZR_LORA * COL + LORA_COLS
N_GATE = N_TILES - TILE_GATE


def _tile_cols(j):
    if j <= TILE_LORA:
        return slice(j * COL, j * COL + (LORA_COLS if j == TILE_LORA else COL))
    start = TILE_LORA * COL + LORA_COLS + (j - TILE_GATE) * COL
    return slice(start, start + COL)
Q_SCALE = HD_A ** -0.5 * math.log2(math.e)


def _rope_apply(y, cos, sin):
    n = y.shape[1]
    lane = lax.broadcasted_iota(jnp.int32, y.shape, 1)
    quarter = HD_A // 4
    first = (lane & (2 * quarter - 1)) < quarter
    swapped = jnp.where(first, pltpu.roll(y, n - quarter, 1), pltpu.roll(y, quarter, 1))
    reps = n // LANES
    cos_f = jnp.concatenate([cos] * reps, axis=1)
    sin_f = jnp.concatenate([sin] * reps, axis=1)
    return y * cos_f + swapped * sin_f


def _inproj_kernel(*refs, latent):
    if latent:
        x_ref, mod_ref, nw_ref, w_ref, qkw_ref, cos_ref, sin_ref, qkv_ref, zr_ref, gate_ref = refs
    else:
        x_ref, mod_ref, nw_ref, w_ref, qkw_ref, qkv_ref, zr_ref, gate_ref, k32_ref, v32_ref = refs
    d = x_ref.shape[2]
    x = x_ref[0]
    ms = jnp.mean(x * x, axis=-1, keepdims=True)
    y = x * lax.rsqrt(ms + EPS_RMS) * nw_ref[...]
    h = (y * (1.0 + mod_ref[0, :, d:2 * d]) + mod_ref[0, :, 0:d]).astype(BF16)

    def tile(j):
        return jnp.dot(h, w_ref[:, _tile_cols(j)], preferred_element_type=F32)

    def qk_norm(zz, w_row):
        gms = _half_block_sum(zz * zz) * (1.0 / HD_A)
        return zz * lax.rsqrt(gms + EPS_RMS) * w_row

    def maybe_rope(v):
        return _rope_apply(v, cos_ref[...], sin_ref[...]) if latent else v

    def col(j):
        return slice(j * COL, (j + 1) * COL)

    order = [TILE_Q, TILE_K, TILE_V] + list(range(TILE_GATE, N_TILES)) + list(range(TILE_RR, TILE_GATE))
    z_next = tile(order[0])
    for pos, j in enumerate(order):
        z = z_next
        if pos + 1 < N_TILES:
            z_next = tile(order[pos + 1])
        if j == TILE_Q:
            qkv_ref[0, :, col(0)] = (maybe_rope(qk_norm(z, qkw_ref[0:1, :])) * Q_SCALE).astype(BF16)
        elif j == TILE_K:
            kn = qk_norm(z, qkw_ref[1:2, :])
            qkv_ref[0, :, col(1)] = maybe_rope(kn).astype(BF16)
            if not latent:
                k32_ref[0] = kn
        elif j == TILE_V:
            qkv_ref[0, :, col(2)] = z.astype(BF16)
            if not latent:
                v32_ref[0] = z
        elif j < TILE_LORA:
            zr_ref[0, :, col(j - TILE_RR)] = z
        elif j == TILE_LORA:
            zr_ref[0, :, ZR_LORA * COL:ZR_COLS] = z
        else:
            gate_ref[0, :, col(j - TILE_GATE)] = _sigmoid(z)


def _inproj(x, mod, norm1_w, w_bf, qkw, rope, *, tm):
    nb, t_len, d = x.shape
    latent = rope is not None
    ncols = w_bf.shape[1]
    in_specs = [pl.BlockSpec((1, tm, d), lambda b, i: (b, i, 0)),
                pl.BlockSpec((1, 1, mod.shape[2]), lambda b, i: (b, 0, 0)),
                pl.BlockSpec((1, d), lambda b, i: (0, 0)),
                pl.BlockSpec((d, ncols), lambda b, i: (0, 0), pipeline_mode=pl.Buffered(1)),
                pl.BlockSpec((2, COL), lambda b, i: (0, 0))]
    args = [x, mod, norm1_w.reshape(1, d), w_bf, qkw]
    if latent:
        in_specs += [pl.BlockSpec((tm, LANES), lambda b, i: (i, 0))] * 2
        args += list(rope)
    widths = [3 * COL, ZR_COLS, N_GATE * COL] + ([] if latent else [COL, COL])
    dtypes = [BF16, F32, F32] + ([] if latent else [F32, F32])
    out_specs = [pl.BlockSpec((1, tm, w), lambda b, i: (b, i, 0)) for w in widths]
    out_shape = [jax.ShapeDtypeStruct((nb, t_len, w), dt) for w, dt in zip(widths, dtypes)]
    return pl.pallas_call(
        functools.partial(_inproj_kernel, latent=latent),
        grid=(nb, t_len // tm),
        in_specs=in_specs,
        out_specs=out_specs,
        out_shape=out_shape,
        compiler_params=pltpu.CompilerParams(dimension_semantics=("parallel", "parallel"),
                                             vmem_limit_bytes=RESIDENT_VMEM_LIMIT),
        name="inproj_latent" if latent else "inproj_context",
    )(*args)


def _attn_kernel(*refs, has_cache, lam_init):
    if has_cache:
        q_ref, k_ref, v_ref, ck_ref, cv_ref, lam_ref, sw_ref, o_ref = refs
    else:
        q_ref, k_ref, v_ref, lam_ref, sw_ref, o_ref = refs
    tq = q_ref.shape[1]
    t_k = k_ref.shape[1]
    lp = lam_ref[...]
    lam = (jnp.exp(jnp.sum(lp[0:1] * lp[1:2], axis=-1, keepdims=True))
           - jnp.exp(jnp.sum(lp[2:3] * lp[3:4], axis=-1, keepdims=True)) + lam_init)

    chunks = [(ck_ref, cv_ref, slice(None))] if has_cache else []
    kc = min(KV_CHUNK, t_k)
    chunks += [(k_ref, v_ref, slice(c * kc, (c + 1) * kc)) for c in range(t_k // kc)]

    heads = []
    for h in range(q_ref.shape[2] // LANES):
        hs = slice(h * LANES, (h + 1) * LANES)
        q = q_ref[0, :, hs].astype(F32)
        lane = lax.broadcasted_iota(jnp.int32, q.shape, 1)
        qs = jnp.concatenate([jnp.where(lane < HD_A, q, 0.0), jnp.where(lane >= HD_A, q, 0.0)],
                             axis=0).astype(BF16)
        heads.append(dict(hs=hs, qs=qs, m=None, acc=None))

    def scores(hd, c):
        kr, _, rs = chunks[c]
        return lax.dot_general(hd['qs'], kr[0, rs, hd['hs']], NT, preferred_element_type=F32)

    for hd in heads:
        hd['s_next'] = scores(hd, 0)
    for c, (_, vr, rs) in enumerate(chunks):
        for hd in heads:
            s = hd['s_next']
            if c + 1 < len(chunks):
                hd['s_next'] = scores(hd, c + 1)
            mx = jnp.max(s, axis=-1, keepdims=True)
            m, acc = hd['m'], hd['acc']
            m_new = mx if m is None else jnp.maximum(m, mx)
            p = jnp.exp2(s - m_new)
            v_c = vr[0, rs, hd['hs']]
            v_aug = jnp.concatenate([v_c, jnp.ones_like(v_c)], axis=1)
            pv = jnp.dot(p.astype(BF16), v_aug, preferred_element_type=F32)
            hd['acc'] = pv if m is None else jnp.exp2(m - m_new) * acc + pv
            hd['m'] = m_new
    for hd in heads:
        acc = hd['acc']
        o = acc[:, 0:VD_A] / acc[:, VD_A:2 * VD_A]
        o = o[0:tq] - lam * o[tq:2 * tq]
        ms = jnp.mean(o * o, axis=-1, keepdims=True)
        o_ref[0, :, hd['hs']] = (o * lax.rsqrt(ms + EPS_RMS) * sw_ref[...] * (1.0 - lam_init)).astype(o_ref.dtype)


def _attention(qkv, cache, lam_p, subln_w, lam_init, *, tq, heads_per_step):
    nb, t_len, _ = qkv.shape
    hw = heads_per_step * LANES
    n_hg = N_HEADS_A // heads_per_step
    in_specs = [pl.BlockSpec((1, tq, hw), lambda b, h, i: (b, i, h)),
                pl.BlockSpec((1, t_len, hw), lambda b, h, i: (b, 0, n_hg + h)),
                pl.BlockSpec((1, t_len, hw), lambda b, h, i: (b, 0, 2 * n_hg + h))]
    args = [qkv, qkv, qkv]
    if cache is not None:
        ck, cv = cache
        past = ck.shape[1]
        in_specs += [pl.BlockSpec((1, past, hw), lambda b, h, i: (b, 0, h))] * 2
        args += [ck, cv]
    in_specs += [pl.BlockSpec((4, HD_A), lambda b, h, i: (0, 0)),
                 pl.BlockSpec((1, VD_A), lambda b, h, i: (0, 0))]
    args += [lam_p, subln_w.reshape(1, VD_A)]
    return pl.pallas_call(
        functools.partial(_attn_kernel, has_cache=cache is not None, lam_init=lam_init),
        grid=(nb, n_hg, t_len // tq),
        in_specs=in_specs,
        out_specs=pl.BlockSpec((1, tq, hw), lambda b, h, i: (b, i, h)),
        out_shape=jax.ShapeDtypeStruct((nb, t_len, N_HEADS_A * VD_A), BF16),
        compiler_params=_cparams(("parallel", "parallel", "arbitrary")),
        name="attention_latent" if cache is not None else "attention_context",
    )(*args)


def _prep_kernel(r_ref, k_ref, la_ref, wup_ref, aup_ref, w0_ref, a0_ref, kk_ref_p, ka_ref, rk_ref,
                 kk_out, lw_out, a_out, bs_out):
    r = r_ref[0]
    k = k_ref[0]
    la = la_ref[0]
    ones_bd = _group_ones(LANES, HD_R)
    kraw = k * kk_ref_p[...]
    nrm = jnp.sqrt(_group_sum(kraw * kraw, ones_bd))
    kk = kraw / jnp.maximum(nrm, 1e-12)
    kk_out[0] = kk
    wl = jnp.tanh(la[:, 0:2 * LORA])
    al = la[:, 2 * LORA:4 * LORA]
    lane = lax.broadcasted_iota(jnp.int32, wl.shape, 1)
    rkd = None
    for z in range(2):
        sel = _idiv(lane, LORA) == z
        w = w0_ref[z:z + 1, :] + _dot_x3(jnp.where(sel, wl, 0.0), wup_ref[...])
        lw_out[z, 0] = -math.exp(-0.5) * _sigmoid(w)
        a = _sigmoid(a0_ref[z:z + 1, :] + _dot_x3(jnp.where(sel, al, 0.0), aup_ref[...]))
        a_out[z, 0] = a
        kd = k * (1.0 + (a - 1.0) * ka_ref[...])
        rkd = r * kd if rkd is None else rkd + r * kd
    bs_out[0] = _group_sum(rkd * rk_ref[...], ones_bd)


def _rwkv_prep(z, w_lora_up, a_lora_up, w0, a0, k_k, k_a, r_k, *, tm):
    nb, t_len, _ = z.shape
    tok = lambda col: pl.BlockSpec((1, tm, COL), lambda b, i: (b, i, col))
    par = lambda rows: pl.BlockSpec((rows, W_R), lambda b, i: (0, 0))
    dir_spec = pl.BlockSpec((2, 1, tm, W_R), lambda b, i: (0, b, i, 0))
    tok_out = pl.BlockSpec((1, tm, W_R), lambda b, i: (b, i, 0))
    one = jax.ShapeDtypeStruct((nb, t_len, W_R), F32)
    two = jax.ShapeDtypeStruct((2, nb, t_len, W_R), F32)
    return pl.pallas_call(
        _prep_kernel,
        grid=(nb, t_len // tm),
        in_specs=[tok(ZR_RR), tok(ZR_KR),
                  pl.BlockSpec((1, tm, 4 * LORA), lambda b, i: (b, i, ZR_LORA * COL // (4 * LORA))),
                  par(2 * LORA), par(2 * LORA), par(2), par(2), par(1), par(1), par(1)],
        out_specs=[tok_out, dir_spec, dir_spec, tok_out],
        out_shape=[one, two, two, one],
        compiler_params=_cparams(("parallel", "parallel")),
        name="rwkv_prep",
    )(z, z, z, w_lora_up.reshape(2 * LORA, W_R), a_lora_up.reshape(2 * LORA, W_R), w0, a0,
      k_k.reshape(1, W_R), k_a.reshape(1, W_R), r_k.reshape(1, W_R))


def _pair_rows(y):
    lane = lax.broadcasted_iota(jnp.int32, y.shape, 1) & (LANES - 1)
    return jnp.concatenate([jnp.where(lane < HD_R, y, 0.0).astype(BF16),
                            jnp.where(lane >= HD_R, y, 0.0).astype(BF16)], axis=0)


def _rwkv_chunk_kernel(rf, kf, kkf, vf, lwf, af, rb, kb, kkb, vb, lwb, ab, ka_ref, *rest, nbb, zero_init):
    if zero_init:
        yf_ref, yb_ref, sfin_ref, s_scr = rest
    else:
        s0_ref, yf_ref, yb_ref, sfin_ref, s_scr = rest
    c = pl.program_id(1)
    n_pairs = W_R // LANES

    @pl.when(c == 0)
    def _():
        s_scr[...] = jnp.zeros_like(s_scr) if zero_init else s0_ref[...]

    C = CHUNK
    t_i = lax.broadcasted_iota(jnp.int32, (C, C), 0)
    i_i = lax.broadcasted_iota(jnp.int32, (C, C), 1)
    t2 = lax.broadcasted_iota(jnp.int32, (C, LANES), 0)
    i2 = lax.broadcasted_iota(jnp.int32, (C, LANES), 1) & (HD_R - 1)
    same_sub = _idiv(t2, SUB) == _idiv(i2, SUB)
    eye = lax.broadcasted_iota(jnp.int32, (LANES, LANES), 0) == lax.broadcasted_iota(jnp.int32, (LANES, LANES), 1)
    same_head = (_idiv(lax.broadcasted_iota(jnp.int32, (LANES, LANES), 0), HD_R)
                 == _idiv(lax.broadcasted_iota(jnp.int32, (LANES, LANES), 1), HD_R))
    masks = {}
    for rev in (False, True):
        tri = (i_i >= t_i) if rev else (i_i <= t_i)
        masks[rev] = dict(tri=jnp.where(tri, 1.0, 0.0).astype(BF16),
                          strict=(i2 > t2) if rev else (i2 < t2),
                          incl=(i2 >= t2) if rev else (i2 <= t2))

    chains = []
    for bi in range(nbb):
        for rev, (r_, k_, kk_, v_, lw_, a_) in ((False, (rf, kf, kkf, vf, lwf, af)),
                                                 (True, (rb, kb, kkb, vb, lwb, ab))):
            for p in range(n_pairs):
                sl = slice(p * LANES, (p + 1) * LANES)
                a = a_[bi, :, sl]
                kk = kk_[bi, :, sl]
                chains.append(dict(bi=bi, rev=rev, p=p, r=r_[bi, :, sl], kk=kk, v=v_[bi, :, sl], lw=lw_[bi, :, sl],
                                   b=kk * a, kd=k_[bi, :, sl] * (1.0 + (a - 1.0) * ka_ref[:, sl])))

    for ch in chains:
        h, m, l = _split3(ch['lw'])
        cl = jnp.dot(masks[ch['rev']]['tri'], jnp.concatenate([h, m, l], axis=1), preferred_element_type=F32)
        ch['cl'] = cl[:, 0:LANES] + cl[:, LANES:2 * LANES] + cl[:, 2 * LANES:3 * LANES]
    for ch in chains:
        cl, lw = ch['cl'], ch['lw']
        p_in = jnp.exp(cl)
        p_inv = jnp.exp(-cl)
        p_ex = jnp.exp(cl - lw)
        ch['pc'] = jnp.exp(jnp.sum(lw, axis=0, keepdims=True))
        p_end = ch['pc'] * p_inv
        ch['at'] = -ch['kk'] * p_ex
        ch['rt'] = ch['r'] * p_in
        ch['bt'] = ch['b'] * p_inv
        ch['kt'] = ch['kd'] * p_inv
        ch['bh'] = ch['b'] * p_end
        ch['kh'] = ch['kd'] * p_end
        ch['S'] = s_scr[ch['bi'], int(ch['rev']), ch['p']]
    for ch in chains:
        g = lax.dot_general(jnp.concatenate([ch['at'], ch['rt']], axis=0).astype(BF16),
                            jnp.concatenate([_pair_rows(ch['bt']), _pair_rows(ch['kt'])], axis=0),
                            NT, preferred_element_type=F32)
        mk = masks[ch['rev']]
        ch['lab'] = jnp.where(mk['strict'], g[0:C, 0:LANES], 0.0)
        ch['lak'] = jnp.where(mk['strict'], g[0:C, LANES:2 * LANES], 0.0)
        ch['lrb'] = jnp.where(mk['incl'], g[C:2 * C, 0:LANES], 0.0)
        ch['lrk'] = jnp.where(mk['incl'], g[C:2 * C, LANES:2 * LANES], 0.0)
        ch['vbd'] = _pair_rows(ch['v'])
    for ch in chains:
        x0 = jnp.dot(jnp.concatenate([ch['at'], ch['lak']], axis=1).astype(BF16),
                     jnp.concatenate([ch['S'].astype(BF16), ch['vbd']], axis=0), preferred_element_type=F32)
        nd = jnp.where(same_sub, ch['lab'], 0.0)
        no = jnp.where(same_sub, 0.0, ch['lab'])
        ch['zc'] = jnp.concatenate([x0, no], axis=1)
        ch['nj'] = nd
    n1 = int(math.log2(SUB))
    for lev in range(n1):
        for ch in chains:
            if lev < n1 - 1:
                rr = jnp.dot(ch['nj'].astype(BF16), _pair_rows(jnp.concatenate([ch['zc'], ch['nj']], axis=1)),
                             preferred_element_type=F32)
                ch['zc'] = ch['zc'] + rr[:, 0:2 * LANES]
                ch['nj'] = rr[:, 2 * LANES:3 * LANES]
            else:
                ch['zc'] = ch['zc'] + jnp.dot(ch['nj'].astype(BF16), _pair_rows(ch['zc']),
                                              preferred_element_type=F32)
    for ch in chains:
        ch['x'] = ch['zc'][:, 0:LANES]
        ch['mj'] = ch['zc'][:, LANES:2 * LANES]
    n2 = int(math.log2(C // SUB))
    for lev in range(n2):
        for ch in chains:
            if lev < n2 - 1:
                rr = jnp.dot(ch['mj'].astype(BF16), _pair_rows(jnp.concatenate([ch['x'], ch['mj']], axis=1)),
                             preferred_element_type=F32)
                ch['x'] = ch['x'] + rr[:, 0:LANES]
                ch['mj'] = rr[:, LANES:2 * LANES]
            else:
                ch['x'] = ch['x'] + jnp.dot(ch['mj'].astype(BF16), _pair_rows(ch['x']), preferred_element_type=F32)
    for ch in chains:
        u = ch['x']
        ubd = _pair_rows(u)
        y = jnp.dot(jnp.concatenate([ch['rt'], ch['lrb'], ch['lrk']], axis=1).astype(BF16),
                    jnp.concatenate([ch['S'].astype(BF16), ubd, ch['vbd']], axis=0), preferred_element_type=F32)
        dg = jnp.where(eye, jnp.broadcast_to(ch['pc'], (LANES, LANES)), 0.0)
        s_new = lax.dot_general(jnp.concatenate([ch['bh'], ch['kh'], dg], axis=0).astype(BF16),
                                jnp.concatenate([u, ch['v'], ch['S']], axis=0).astype(BF16),
                                TN, preferred_element_type=F32)
        s_new = jnp.where(same_head, s_new, 0.0)
        sl = slice(ch['p'] * LANES, (ch['p'] + 1) * LANES)
        if ch['rev']:
            yb_ref[ch['bi'], :, sl] = y
        else:
            yf_ref[ch['bi'], :, sl] = y
        s_scr[ch['bi'], int(ch['rev']), ch['p']] = s_new

    @pl.when(c == pl.num_programs(1) - 1)
    def _():
        sfin_ref[...] = s_scr[...]


def _rwkv_scan(r_src, r_col, k_col, v_col, kk, lw, aa, k_a, s0, *, nbb):
    nb, t_len, _ = kk.shape
    nc = t_len // CHUNK
    n_pairs = W_R // LANES

    def tok(col, rev):
        return pl.BlockSpec((nbb, CHUNK, W_R), (lambda g, c: (g, nc - 1 - c, col)) if rev else (lambda g, c: (g, c, col)))

    def dirs(z, rev):
        return pl.BlockSpec((None, nbb, CHUNK, W_R),
                            (lambda g, c: (z, g, nc - 1 - c, 0)) if rev else (lambda g, c: (z, g, c, 0)))

    st_spec = pl.BlockSpec((nbb, 2, n_pairs, LANES, LANES), lambda g, c: (g, 0, 0, 0, 0))
    in_specs, args = [], []
    for rev in (False, True):
        z = int(rev)
        in_specs += [tok(r_col, rev), tok(k_col, rev), tok(0, rev), tok(v_col, rev), dirs(z, rev), dirs(z, rev)]
        args += [r_src, r_src, kk, r_src, lw, aa]
    in_specs.append(pl.BlockSpec((1, W_R), lambda g, c: (0, 0)))
    args.append(k_a.reshape(1, W_R))
    if s0 is not None:
        in_specs.append(st_spec)
        args.append(s0)
    yshape = jax.ShapeDtypeStruct((nb, t_len, W_R), F32)
    return pl.pallas_call(
        functools.partial(_rwkv_chunk_kernel, nbb=nbb, zero_init=s0 is None),
        grid=(nb // nbb, nc),
        in_specs=in_specs,
        out_specs=[tok(0, False), tok(0, True), st_spec],
        out_shape=[yshape, yshape, jax.ShapeDtypeStruct((nb, 2, n_pairs, LANES, LANES), F32)],
        scratch_shapes=[pltpu.VMEM((nbb, 2, n_pairs, LANES, LANES), F32)],
        compiler_params=_cparams(("parallel", "arbitrary")),
        name="rwkv_scan",
    )(*args)


def _post_kernel(x_ref, yf_ref, yb_ref, bs_ref, vr_ref, gr_ref, oa_ref, ga_ref, gg_ref, mod_ref,
                 lnw_ref, lnb_ref, wa_ref, wr_ref, wo_ref, n2_ref, wi_ref, wo2_ref, o_ref, *, n_chunks, n_sub):
    d = x_ref.shape[2]
    rows = x_ref.shape[1] // n_sub
    d_ff = wo2_ref.shape[0]
    tf = d_ff // n_chunks
    subs = [dict(rs=slice(i * rows, (i + 1) * rows)) for i in range(n_sub)]

    for sb in subs:
        rs = sb['rs']
        y = yf_ref[0, rs, :] + yb_ref[0, rs, :]
        mu = _half_block_sum(y) * (1.0 / HD_R)
        yc = y - mu
        var = _half_block_sum(yc * yc) * (1.0 / HD_R)
        yn = yc * lax.rsqrt(var + EPS_GN) * lnw_ref[...] + lnb_ref[...]
        sb['o_r'] = (yn + bs_ref[0, rs, :] * vr_ref[0, rs, :]) * _sigmoid(gr_ref[0, rs, :])
    for sb in subs:
        rs = sb['rs']
        sb['merged'] = (ga_ref[0, rs, :] * _dot(oa_ref[0, rs, :], wa_ref[...])
                        + gg_ref[0, rs, :] * _dot(sb['o_r'], wr_ref[...]))
    for sb in subs:
        x1 = x_ref[0, sb['rs'], :] + mod_ref[0, :, 2 * d:3 * d] * _dot(sb['merged'], wo_ref[...])
        ms = jnp.mean(x1 * x1, axis=-1, keepdims=True)
        sb['x1'] = x1
        sb['h'] = (x1 * lax.rsqrt(ms + EPS_RMS) * n2_ref[...] * (1.0 + mod_ref[0, :, 4 * d:5 * d])
                   + mod_ref[0, :, 3 * d:4 * d]).astype(BF16)

    def up(sb, c):
        u = jnp.dot(sb['h'], wi_ref[:, c * tf:(c + 1) * tf], preferred_element_type=F32)
        g = jnp.dot(sb['h'], wi_ref[:, d_ff + c * tf:d_ff + (c + 1) * tf], preferred_element_type=F32)
        return u, g

    for sb in subs:
        sb['nxt'] = up(sb, 0)
        sb['acc'] = None
    for c in range(n_chunks):
        for sb in subs:
            u, g = sb['nxt']
            if c + 1 < n_chunks:
                sb['nxt'] = up(sb, c + 1)
            a = ((u * _sigmoid(u)) * g).astype(BF16)
            part = jnp.dot(a, wo2_ref[c * tf:(c + 1) * tf, :], preferred_element_type=F32)
            sb['acc'] = part if sb['acc'] is None else sb['acc'] + part
    for sb in subs:
        o_ref[0, sb['rs'], :] = sb['x1'] + mod_ref[0, :, 5 * d:6 * d] * sb['acc']


def _post(x, zr, gates, yf, yb, bsum, oa, mod, ln_x_w, ln_x_b, wa_bf, wr_bf, wo_bf, norm2_w, w_in_bf, w_out_bf,
          *, tm, n_chunks, n_sub):
    nb, t_len, d = x.shape
    tok = lambda w, col: pl.BlockSpec((1, tm, w), lambda b, i: (b, i, col))
    resident = lambda a: pl.BlockSpec(a.shape, lambda b, i: (0,) * a.ndim, pipeline_mode=pl.Buffered(1))
    lnw, lnb, n2 = ln_x_w.reshape(1, W_R), ln_x_b.reshape(1, W_R), norm2_w.reshape(1, d)
    return pl.pallas_call(
        functools.partial(_post_kernel, n_chunks=n_chunks, n_sub=n_sub),
        grid=(nb, t_len // tm),
        in_specs=[tok(d, 0), tok(W_R, 0), tok(W_R, 0), tok(W_R, 0), tok(COL, ZR_VR), tok(COL, ZR_GR),
                  tok(W_R, 0), tok(d, 0), tok(d, 1),
                  pl.BlockSpec((1, 1, mod.shape[2]), lambda b, i: (b, 0, 0)),
                  resident(lnw), resident(lnb), resident(wa_bf), resident(wr_bf), resident(wo_bf), resident(n2),
                  resident(w_in_bf), resident(w_out_bf)],
        out_specs=tok(d, 0),
        out_shape=jax.ShapeDtypeStruct((nb, t_len, d), F32),
        compiler_params=pltpu.CompilerParams(dimension_semantics=("parallel", "parallel"),
                                             vmem_limit_bytes=RESIDENT_VMEM_LIMIT),
        name="post",
    )(x, yf, yb, bsum, zr, zr, oa, gates, gates, mod, lnw, lnb, wa_bf, wr_bf, wo_bf, n2, w_in_bf, w_out_bf)


def _state_to_pairs(s):
    nb = s.shape[0]
    st = jnp.swapaxes(s, -1, -2).reshape(nb, 2, N_HEADS_R // 2, 2, HD_R, HD_R)
    zero = jnp.zeros_like(st[:, :, :, 0])
    top = jnp.concatenate([st[:, :, :, 0], zero], axis=-1)
    bot = jnp.concatenate([zero, st[:, :, :, 1]], axis=-1)
    return jnp.concatenate([top, bot], axis=-2)


def _pairs_to_state(sp):
    nb = sp.shape[0]
    h0 = sp[:, :, :, 0:HD_R, 0:HD_R]
    h1 = sp[:, :, :, HD_R:, HD_R:]
    st = jnp.stack([h0, h1], axis=3).reshape(nb, 2, N_HEADS_R, HD_R, HD_R)
    return jnp.swapaxes(st, -1, -2)


def _layer(x_tok, nb_seq, mod, lp, lam_init, cache, s0_pairs, rope):
    nbm, tmod, d = x_tok.shape
    t_seq = nbm * tmod // nb_seq
    plan = _tile_plan(t_seq)
    outs = _inproj(x_tok, mod, lp['norm1_w'], lp['w_in'], lp['qkw'], rope, tm=plan['tm'])
    qkv, zr, gates = outs[:3]
    kv = outs[3:5] if cache is None else None
    seq = lambda a: a.reshape(a.shape[:-3] + (nb_seq, t_seq, a.shape[-1]))
    oa = _attention(seq(qkv), cache, lp['lam'], lp['subln_w'], lam_init, tq=plan['tq'],
                    heads_per_step=plan['heads_per_step'])
    kk, lw, aa, bsum = _rwkv_prep(zr, lp['w_lora_up'], lp['a_lora_up'], lp['w0'], lp['a0'],
                                  lp['k_k'], lp['k_a'], lp['r_k'], tm=plan['tm'])
    yf, yb, s_fin = _rwkv_scan(seq(zr), ZR_RR, ZR_KR, ZR_VR, seq(kk), seq(lw), seq(aa), lp['k_a'], s0_pairs,
                               nbb=plan['nbb'])
    tokv = lambda a: a.reshape(nbm, tmod, a.shape[-1])
    y = _post(x_tok, zr, gates, tokv(yf), tokv(yb), bsum, tokv(oa), mod, lp['ln_x_w'], lp['ln_x_b'],
              lp['w_attn_br'], lp['w_rwkv_br'], lp['w_out'], lp['norm2_w'], lp['w_ffn_in'], lp['w_ffn_out'],
              tm=plan['tm'], n_chunks=plan['n_chunks'], n_sub=plan['n_sub'])
    return y, kv, s_fin


def kernel(x_prompt, x_sample, cache_k, cache_v, state_rwkv, c, c_ctx, ada_w, ada_b, norm1_w, norm2_w, w_in, q_norm_w, k_norm_w, lambda_q1, lambda_k1, lambda_q2, lambda_k2, subln_w, w_lora_up, w0, a_lora_up, a0, k_k, k_a, r_k, ln_x_w, ln_x_b, w_attn_br, w_rwkv_br, w_out, w_ffn_in, w_ffn_out):
    depth = ada_w.shape[0]
    batch, seq_len, d = x_prompt.shape
    dec_batch, dec_seq, _ = x_sample.shape
    past = cache_k.shape[2]
    qk_cols = N_HEADS_A * 2 * HD_A
    assert qk_cols == COL and W_R == COL and N_HEADS_A * VD_A == COL

    rope = _rope_tables(dec_seq)
    y_prompt = x_prompt.reshape(1, batch * seq_len, d)
    y_sample = x_sample
    ks_out, vs_out, ss_out = [], [], []
    cond_rows = 16
    for li in range(depth):
        cond = jnp.concatenate([c_ctx[None, :], c, jnp.zeros((cond_rows - 1 - dec_batch, d), F32)], axis=0)
        m = _modulation(cond, ada_w[li], ada_b[li])
        mod_ctx = m[0:1].reshape(1, 1, 6 * d)
        mod_lat = m[1:1 + dec_batch].reshape(dec_batch, 1, 6 * d)

        assert w_in.shape[2] == _tile_cols(N_TILES - 1).stop
        lp = {
            'norm1_w': norm1_w[li], 'norm2_w': norm2_w[li], 'w_in': w_in[li].astype(BF16),
            'qkw': jnp.stack([jnp.tile(q_norm_w[li], 2 * N_HEADS_A), jnp.tile(k_norm_w[li], 2 * N_HEADS_A)]),
            'lam': jnp.stack([lambda_q1[li], lambda_k1[li], lambda_q2[li], lambda_k2[li]]),
            'subln_w': subln_w[li], 'w_lora_up': w_lora_up[li], 'w0': w0[li], 'a_lora_up': a_lora_up[li],
            'a0': a0[li], 'k_k': k_k[li], 'k_a': k_a[li], 'r_k': r_k[li], 'ln_x_w': ln_x_w[li],
            'ln_x_b': ln_x_b[li], 'w_attn_br': w_attn_br[li].astype(BF16), 'w_rwkv_br': w_rwkv_br[li].astype(BF16),
            'w_out': w_out[li].astype(BF16), 'w_ffn_in': w_ffn_in[li].astype(BF16),
            'w_ffn_out': w_ffn_out[li].astype(BF16),
        }
        lam_init = 0.8 - 0.6 * math.exp(-0.3 * li)

        y_prompt, kv_ctx, s_ctx = _layer(y_prompt, batch, mod_ctx, lp, lam_init, None, None, None)
        ks_out.append(kv_ctx[0].reshape(batch, seq_len, N_HEADS_A, 2, HD_A))
        vs_out.append(kv_ctx[1].reshape(batch, seq_len, N_HEADS_A, VD_A))
        ss_out.append(_pairs_to_state(s_ctx))

        cache = (cache_k[:, li].reshape(dec_batch, past, COL).astype(BF16),
                 cache_v[:, li].reshape(dec_batch, past, COL).astype(BF16))
        s0_lat = _state_to_pairs(state_rwkv[:, li])
        y_sample, _, _ = _layer(y_sample, dec_batch, mod_lat, lp, lam_init, cache, s0_lat, rope)
    new_k = jnp.stack(ks_out, axis=1)
    new_v = jnp.stack(vs_out, axis=1)
    new_state = jnp.stack(ss_out, axis=1)
    return (y_prompt.reshape(batch, seq_len, d), y_sample, new_k, new_v, new_state)
```

```python
import functools
import math

import jax
import jax.numpy as jnp
from jax import lax
from jax.experimental import pallas as pl
from jax.experimental.pallas import tpu as pltpu

F32 = jnp.float32
BF16 = jnp.bfloat16

N_HEADS_A = 4
HD_A = 64
VD_A = 2 * HD_A
N_HEADS_R = 8
HD_R = 64
W_R = N_HEADS_R * HD_R
LORA = 64
GRID_W = 64
ROPE_THETA = 10000.0
EPS_RMS = 1e-6
EPS_GN = 64e-5

LANES = 128
VMEM_BYTES_V7X = 64 * 1024 * 1024
VMEM_LIMIT = 48 * 1024 * 1024
RESIDENT_VMEM_LIMIT = VMEM_BYTES_V7X - 6 * 1024 * 1024

CHUNK = 64
SUB = 8
COL = 512
KV_CHUNK = 512


def _tile_plan(seq_len):
    short = seq_len <= KV_CHUNK
    return dict(
        tm=512,
        tq=min(seq_len, 512),
        heads_per_step=N_HEADS_A if short else 1,
        nbb=4,
        n_chunks=4,
        n_sub=2,
    )

NN = (((1,), (0,)), ((), ()))
NT = (((1,), (1,)), ((), ()))
TN = (((0,), (0,)), ((), ()))


def _dot(a, b, dims=NN):
    return lax.dot_general(a.astype(BF16), b.astype(BF16), dims, preferred_element_type=F32)


def _split2(x):
    hi = x.astype(BF16)
    lo = (x - hi.astype(F32)).astype(BF16)
    return hi, lo


def _split3(x):
    hi = x.astype(BF16)
    r1 = x - hi.astype(F32)
    mid = r1.astype(BF16)
    lo = (r1 - mid.astype(F32)).astype(BF16)
    return hi, mid, lo


def _dot_x3(a, b, dims=NN):
    ah, al = _split2(a)
    bh, bl = _split2(b)
    d = functools.partial(lax.dot_general, dimension_numbers=dims, preferred_element_type=F32)
    return d(ah, bh) + d(ah, bl) + d(al, bh)


def _sigmoid(x):
    return 1.0 / (1.0 + jnp.exp(-x))


def _idiv(x, pow2):
    assert pow2 & (pow2 - 1) == 0
    return x >> int(math.log2(pow2))


def _group_ones(n, group):
    r = _idiv(lax.broadcasted_iota(jnp.int32, (n, n), 0), group)
    c = _idiv(lax.broadcasted_iota(jnp.int32, (n, n), 1), group)
    return jnp.where(r == c, 1.0, 0.0).astype(BF16)


def _group_sum(x, ones_bd):
    ones2 = jnp.concatenate([ones_bd, ones_bd], axis=0)
    outs = []
    for cb in range(x.shape[1] // LANES):
        hi, lo = _split2(x[:, cb * LANES:(cb + 1) * LANES])
        outs.append(jnp.dot(jnp.concatenate([hi, lo], axis=1), ones2, preferred_element_type=F32))
    return outs[0] if len(outs) == 1 else jnp.concatenate(outs, axis=1)


def _half_block_sum(x):
    half = LANES // 2
    outs = []
    for cb in range(x.shape[1] // LANES):
        xb = x[:, cb * LANES:(cb + 1) * LANES]
        low = lax.broadcasted_iota(jnp.int32, xb.shape, 1) < half
        s_lo = jnp.sum(jnp.where(low, xb, 0.0), axis=-1, keepdims=True)
        s_hi = jnp.sum(jnp.where(low, 0.0, xb), axis=-1, keepdims=True)
        outs.append(jnp.where(low, s_lo, s_hi))
    return outs[0] if len(outs) == 1 else jnp.concatenate(outs, axis=1)


def _cparams(sem):
    return pltpu.CompilerParams(dimension_semantics=sem, vmem_limit_bytes=VMEM_LIMIT)


def _mod_kernel(c_ref, w_ref, b_ref, o_ref):
    c = c_ref[...]
    s = c * _sigmoid(c)
    o_ref[...] = _dot_x3(s, w_ref[...]) + b_ref[...]


def _modulation(cond, ada_w, ada_b):
    rows, d = cond.shape
    n = ada_w.shape[1]
    tn = 1536
    return pl.pallas_call(
        _mod_kernel,
        grid=(n // tn,),
        in_specs=[pl.BlockSpec((rows, d), lambda j: (0, 0)),
                  pl.BlockSpec((d, tn), lambda j: (0, j)),
                  pl.BlockSpec((1, tn), lambda j: (0, j))],
        out_specs=pl.BlockSpec((rows, tn), lambda j: (0, j)),
        out_shape=jax.ShapeDtypeStruct((rows, n), F32),
        compiler_params=_cparams(("parallel",)),
        name="modulation",
    )(cond, ada_w, ada_b.reshape(1, n))


def _rope_kernel(cos_ref, sin_ref, *, tm):
    i = pl.program_id(0)
    shift = int(math.log2(GRID_W))
    t = i * tm + lax.broadcasted_iota(jnp.int32, (tm, LANES), 0)
    lane = lax.broadcasted_iota(jnp.int32, (tm, LANES), 1)
    l64 = lane & (HD_A - 1)
    nf = HD_A // 4
    f = (l64 & (nf - 1)).astype(F32)
    inv = jnp.exp(f * (-math.log(ROPE_THETA) / nf))
    pos = jnp.where(l64 < HD_A // 2, t >> shift, t & (GRID_W - 1)).astype(F32)
    ang = pos * inv
    sn = jnp.sin(ang)
    cos_ref[...] = jnp.cos(ang)
    sin_ref[...] = jnp.where((l64 & (2 * nf - 1)) < nf, -sn, sn)


def _rope_tables(t_len):
    assert GRID_W & (GRID_W - 1) == 0
    tm = 512
    return pl.pallas_call(
        functools.partial(_rope_kernel, tm=tm),
        grid=(t_len // tm,),
        out_specs=[pl.BlockSpec((tm, LANES), lambda i: (i, 0))] * 2,
        out_shape=[jax.ShapeDtypeStruct((t_len, LANES), F32)] * 2,
        compiler_params=_cparams(("parallel",)),
        name="rope_tables",
    )()


TILE_Q, TILE_K, TILE_V, TILE_RR, TILE_KR, TILE_VR, TILE_GR, TILE_LORA, TILE_GATE = 0, 1, 2, 3, 4, 5, 6, 7, 8
N_TILES = 12
LORA_COLS = 4 * LORA
ZR_RR, ZR_KR, ZR_VR, ZR_GR, ZR_LORA = 0, 1, 2, 3, 4
ZR_COLS = ZR_LORA * COL + LORA_COLS
N_GATE = N_TILES - TILE_GATE


def _tile_cols(j):
    if j <= TILE_LORA:
        return slice(j * COL, j * COL + (LORA_COLS if j == TILE_LORA else COL))
    start = TILE_LORA * COL + LORA_COLS + (j - TILE_GATE) * COL
    return slice(start, start + COL)
Q_SCALE = HD_A ** -0.5 * math.log2(math.e)


def _rope_apply(y, cos, sin):
    n = y.shape[1]
    lane = lax.broadcasted_iota(jnp.int32, y.shape, 1)
    quarter = HD_A // 4
    first = (lane & (2 * quarter - 1)) < quarter
    swapped = jnp.where(first, pltpu.roll(y, n - quarter, 1), pltpu.roll(y, quarter, 1))
    reps = n // LANES
    cos_f = jnp.concatenate([cos] * reps, axis=1)
    sin_f = jnp.concatenate([sin] * reps, axis=1)
    return y * cos_f + swapped * sin_f


def _inproj_kernel(*refs, latent):
    if latent:
        x_ref, mod_ref, nw_ref, w_ref, qkw_ref, cos_ref, sin_ref, qkv_ref, zr_ref, gate_ref = refs
    else:
        x_ref, mod_ref, nw_ref, w_ref, qkw_ref, qkv_ref, zr_ref, gate_ref, k32_ref, v32_ref = refs
    d = x_ref.shape[2]
    n_sub = 2
    rows = x_ref.shape[1] // n_sub
    subs = []
    for i in range(n_sub):
        rs = slice(i * rows, (i + 1) * rows)
        x = x_ref[0, rs, :]
        ms = jnp.mean(x * x, axis=-1, keepdims=True)
        y = x * lax.rsqrt(ms + EPS_RMS) * nw_ref[...]
        subs.append(dict(rs=rs, h=(y * (1.0 + mod_ref[0, :, d:2 * d]) + mod_ref[0, :, 0:d]).astype(BF16)))

    def tile(sb, j):
        return jnp.dot(sb['h'], w_ref[:, _tile_cols(j)], preferred_element_type=F32)

    def qk_norm(zz, w_row):
        gms = _half_block_sum(zz * zz) * (1.0 / HD_A)
        return zz * lax.rsqrt(gms + EPS_RMS) * w_row

    def maybe_rope(v, rs):
        return _rope_apply(v, cos_ref[rs, :], sin_ref[rs, :]) if latent else v

    def col(j):
        return slice(j * COL, (j + 1) * COL)

    order = [TILE_Q, TILE_K, TILE_V] + list(range(TILE_GATE, N_TILES)) + list(range(TILE_RR, TILE_GATE))
    for sb in subs:
        sb['z_next'] = tile(sb, order[0])
    for pos, j in enumerate(order):
        for sb in subs:
            rs = sb['rs']
            z = sb['z_next']
            if pos + 1 < N_TILES:
                sb['z_next'] = tile(sb, order[pos + 1])
            if j == TILE_Q:
                qkv_ref[0, rs, col(0)] = (maybe_rope(qk_norm(z, qkw_ref[0:1, :]), rs) * Q_SCALE).astype(BF16)
            elif j == TILE_K:
                kn = qk_norm(z, qkw_ref[1:2, :])
                qkv_ref[0, rs, col(1)] = maybe_rope(kn, rs).astype(BF16)
                if not latent:
                    k32_ref[0, rs, :] = kn
            elif j == TILE_V:
                qkv_ref[0, rs, col(2)] = z.astype(BF16)
                if not latent:
                    v32_ref[0, rs, :] = z
            elif j < TILE_LORA:
                zr_ref[0, rs, col(j - TILE_RR)] = z
            elif j == TILE_LORA:
                zr_ref[0, rs, ZR_LORA * COL:ZR_COLS] = z
            else:
                gate_ref[0, rs, col(j - TILE_GATE)] = _sigmoid(z)


def _inproj(x, mod, norm1_w, w_bf, qkw, rope, *, tm):
    nb, t_len, d = x.shape
    latent = rope is not None
    ncols = w_bf.shape[1]
    in_specs = [pl.BlockSpec((1, tm, d), lambda b, i: (b, i, 0)),
                pl.BlockSpec((1, 1, mod.shape[2]), lambda b, i: (b, 0, 0)),
                pl.BlockSpec((1, d), lambda b, i: (0, 0)),
                pl.BlockSpec((d, ncols), lambda b, i: (0, 0), pipeline_mode=pl.Buffered(1)),
                pl.BlockSpec((2, COL), lambda b, i: (0, 0))]
    args = [x, mod, norm1_w.reshape(1, d), w_bf, qkw]
    if latent:
        in_specs += [pl.BlockSpec((tm, LANES), lambda b, i: (i, 0))] * 2
        args += list(rope)
    widths = [3 * COL, ZR_COLS, N_GATE * COL] + ([] if latent else [COL, COL])
    dtypes = [BF16, F32, F32] + ([] if latent else [F32, F32])
    out_specs = [pl.BlockSpec((1, tm, w), lambda b, i: (b, i, 0)) for w in widths]
    out_shape = [jax.ShapeDtypeStruct((nb, t_len, w), dt) for w, dt in zip(widths, dtypes)]
    return pl.pallas_call(
        functools.partial(_inproj_kernel, latent=latent),
        grid=(nb, t_len // tm),
        in_specs=in_specs,
        out_specs=out_specs,
        out_shape=out_shape,
        compiler_params=pltpu.CompilerParams(dimension_semantics=("parallel", "parallel"),
                                             vmem_limit_bytes=RESIDENT_VMEM_LIMIT),
        name="inproj_latent" if latent else "inproj_context",
    )(*args)


def _attn_kernel(*refs, has_cache, lam_init):
    if has_cache:
        q_ref, k_ref, v_ref, ck_ref, cv_ref, lam_ref, sw_ref, o_ref = refs
    else:
        q_ref, k_ref, v_ref, lam_ref, sw_ref, o_ref = refs
    tq = q_ref.shape[1]
    t_k = k_ref.shape[1]
    lp = lam_ref[...]
    lam = (jnp.exp(jnp.sum(lp[0:1] * lp[1:2], axis=-1, keepdims=True))
           - jnp.exp(jnp.sum(lp[2:3] * lp[3:4], axis=-1, keepdims=True)) + lam_init)

    chunks = [(ck_ref, cv_ref, slice(None))] if has_cache else []
    kc = min(KV_CHUNK, t_k)
    chunks += [(k_ref, v_ref, slice(c * kc, (c + 1) * kc)) for c in range(t_k // kc)]

    heads = []
    for h in range(q_ref.shape[2] // LANES):
        hs = slice(h * LANES, (h + 1) * LANES)
        q = q_ref[0, :, hs].astype(F32)
        lane = lax.broadcasted_iota(jnp.int32, q.shape, 1)
        qs = jnp.concatenate([jnp.where(lane < HD_A, q, 0.0), jnp.where(lane >= HD_A, q, 0.0)],
                             axis=0).astype(BF16)
        heads.append(dict(hs=hs, qs=qs, m=None, acc=None))

    def scores(hd, c):
        kr, _, rs = chunks[c]
        return lax.dot_general(hd['qs'], kr[0, rs, hd['hs']], NT, preferred_element_type=F32)

    for hd in heads:
        hd['s_next'] = scores(hd, 0)
    for c, (_, vr, rs) in enumerate(chunks):
        for hd in heads:
            s = hd['s_next']
            if c + 1 < len(chunks):
                hd['s_next'] = scores(hd, c + 1)
            mx = jnp.max(s, axis=-1, keepdims=True)
            m, acc = hd['m'], hd['acc']
            m_new = mx if m is None else jnp.maximum(m, mx)
            p = jnp.exp2(s - m_new)
            v_c = vr[0, rs, hd['hs']]
            v_aug = jnp.concatenate([v_c, jnp.ones_like(v_c)], axis=1)
            pv = jnp.dot(p.astype(BF16), v_aug, preferred_element_type=F32)
            hd['acc'] = pv if m is None else jnp.exp2(m - m_new) * acc + pv
            hd['m'] = m_new
    for hd in heads:
        acc = hd['acc']
        o = acc[:, 0:VD_A] / acc[:, VD_A:2 * VD_A]
        o = o[0:tq] - lam * o[tq:2 * tq]
        ms = jnp.mean(o * o, axis=-1, keepdims=True)
        o_ref[0, :, hd['hs']] = (o * lax.rsqrt(ms + EPS_RMS) * sw_ref[...] * (1.0 - lam_init)).astype(o_ref.dtype)


def _attention(qkv, cache, lam_p, subln_w, lam_init, *, tq, heads_per_step):
    nb, t_len, _ = qkv.shape
    hw = heads_per_step * LANES
    n_hg = N_HEADS_A // heads_per_step
    in_specs = [pl.BlockSpec((1, tq, hw), lambda b, h, i: (b, i, h)),
                pl.BlockSpec((1, t_len, hw), lambda b, h, i: (b, 0, n_hg + h)),
                pl.BlockSpec((1, t_len, hw), lambda b, h, i: (b, 0, 2 * n_hg + h))]
    args = [qkv, qkv, qkv]
    if cache is not None:
        ck, cv = cache
        past = ck.shape[1]
        in_specs += [pl.BlockSpec((1, past, hw), lambda b, h, i: (b, 0, h))] * 2
        args += [ck, cv]
    in_specs += [pl.BlockSpec((4, HD_A), lambda b, h, i: (0, 0)),
                 pl.BlockSpec((1, VD_A), lambda b, h, i: (0, 0))]
    args += [lam_p, subln_w.reshape(1, VD_A)]
    return pl.pallas_call(
        functools.partial(_attn_kernel, has_cache=cache is not None, lam_init=lam_init),
        grid=(nb, n_hg, t_len // tq),
        in_specs=in_specs,
        out_specs=pl.BlockSpec((1, tq, hw), lambda b, h, i: (b, i, h)),
        out_shape=jax.ShapeDtypeStruct((nb, t_len, N_HEADS_A * VD_A), BF16),
        compiler_params=_cparams(("parallel", "parallel", "arbitrary")),
        name="attention_latent" if cache is not None else "attention_context",
    )(*args)


def _prep_kernel(r_ref, k_ref, la_ref, wup_ref, aup_ref, w0_ref, a0_ref, kk_ref_p, ka_ref, rk_ref,
                 kk_out, lw_out, a_out, bs_out):
    r = r_ref[0]
    k = k_ref[0]
    la = la_ref[0]
    ones_bd = _group_ones(LANES, HD_R)
    kraw = k * kk_ref_p[...]
    nrm = jnp.sqrt(_group_sum(kraw * kraw, ones_bd))
    kk = kraw / jnp.maximum(nrm, 1e-12)
    kk_out[0] = kk
    wl = jnp.tanh(la[:, 0:2 * LORA])
    al = la[:, 2 * LORA:4 * LORA]
    lane = lax.broadcasted_iota(jnp.int32, wl.shape, 1)
    rkd = None
    for z in range(2):
        sel = _idiv(lane, LORA) == z
        w = w0_ref[z:z + 1, :] + _dot_x3(jnp.where(sel, wl, 0.0), wup_ref[...])
        lw_out[z, 0] = -math.exp(-0.5) * _sigmoid(w)
        a = _sigmoid(a0_ref[z:z + 1, :] + _dot_x3(jnp.where(sel, al, 0.0), aup_ref[...]))
        a_out[z, 0] = a
        kd = k * (1.0 + (a - 1.0) * ka_ref[...])
        rkd = r * kd if rkd is None else rkd + r * kd
    bs_out[0] = _group_sum(rkd * rk_ref[...], ones_bd)


def _rwkv_prep(z, w_lora_up, a_lora_up, w0, a0, k_k, k_a, r_k, *, tm):
    nb, t_len, _ = z.shape
    tok = lambda col: pl.BlockSpec((1, tm, COL), lambda b, i: (b, i, col))
    par = lambda rows: pl.BlockSpec((rows, W_R), lambda b, i: (0, 0))
    dir_spec = pl.BlockSpec((2, 1, tm, W_R), lambda b, i: (0, b, i, 0))
    tok_out = pl.BlockSpec((1, tm, W_R), lambda b, i: (b, i, 0))
    one = jax.ShapeDtypeStruct((nb, t_len, W_R), F32)
    two = jax.ShapeDtypeStruct((2, nb, t_len, W_R), F32)
    return pl.pallas_call(
        _prep_kernel,
        grid=(nb, t_len // tm),
        in_specs=[tok(ZR_RR), tok(ZR_KR),
                  pl.BlockSpec((1, tm, 4 * LORA), lambda b, i: (b, i, ZR_LORA * COL // (4 * LORA))),
                  par(2 * LORA), par(2 * LORA), par(2), par(2), par(1), par(1), par(1)],
        out_specs=[tok_out, dir_spec, dir_spec, tok_out],
        out_shape=[one, two, two, one],
        compiler_params=_cparams(("parallel", "parallel")),
        name="rwkv_prep",
    )(z, z, z, w_lora_up.reshape(2 * LORA, W_R), a_lora_up.reshape(2 * LORA, W_R), w0, a0,
      k_k.reshape(1, W_R), k_a.reshape(1, W_R), r_k.reshape(1, W_R))


def _pair_rows(y):
    lane = lax.broadcasted_iota(jnp.int32, y.shape, 1) & (LANES - 1)
    return jnp.concatenate([jnp.where(lane < HD_R, y, 0.0).astype(BF16),
                            jnp.where(lane >= HD_R, y, 0.0).astype(BF16)], axis=0)


def _rwkv_chunk_kernel(rf, kf, kkf, vf, lwf, af, rb, kb, kkb, vb, lwb, ab, ka_ref, *rest, nbb, zero_init):
    if zero_init:
        yf_ref, yb_ref, sfin_ref, s_scr = rest
    else:
        s0_ref, yf_ref, yb_ref, sfin_ref, s_scr = rest
    c = pl.program_id(1)
    n_pairs = W_R // LANES

    @pl.when(c == 0)
    def _():
        s_scr[...] = jnp.zeros_like(s_scr) if zero_init else s0_ref[...]

    C = CHUNK
    t_i = lax.broadcasted_iota(jnp.int32, (C, C), 0)
    i_i = lax.broadcasted_iota(jnp.int32, (C, C), 1)
    t2 = lax.broadcasted_iota(jnp.int32, (C, LANES), 0)
    i2 = lax.broadcasted_iota(jnp.int32, (C, LANES), 1) & (HD_R - 1)
    same_sub = _idiv(t2, SUB) == _idiv(i2, SUB)
    eye = lax.broadcasted_iota(jnp.int32, (LANES, LANES), 0) == lax.broadcasted_iota(jnp.int32, (LANES, LANES), 1)
    same_head = (_idiv(lax.broadcasted_iota(jnp.int32, (LANES, LANES), 0), HD_R)
                 == _idiv(lax.broadcasted_iota(jnp.int32, (LANES, LANES), 1), HD_R))
    masks = {}
    for rev in (False, True):
        tri = (i_i >= t_i) if rev else (i_i <= t_i)
        masks[rev] = dict(tri=jnp.where(tri, 1.0, 0.0).astype(BF16),
                          strict=(i2 > t2) if rev else (i2 < t2),
                          incl=(i2 >= t2) if rev else (i2 <= t2))

    chains = []
    for bi in range(nbb):
        for rev, (r_, k_, kk_, v_, lw_, a_) in ((False, (rf, kf, kkf, vf, lwf, af)),
                                                 (True, (rb, kb, kkb, vb, lwb, ab))):
            for p in range(n_pairs):
                sl = slice(p * LANES, (p + 1) * LANES)
                a = a_[bi, :, sl]
                kk = kk_[bi, :, sl]
                chains.append(dict(bi=bi, rev=rev, p=p, r=r_[bi, :, sl], kk=kk, v=v_[bi, :, sl], lw=lw_[bi, :, sl],
                                   b=kk * a, kd=k_[bi, :, sl] * (1.0 + (a - 1.0) * ka_ref[:, sl])))

    for ch in chains:
        h, m, l = _split3(ch['lw'])
        cl = jnp.dot(masks[ch['rev']]['tri'], jnp.concatenate([h, m, l], axis=1), preferred_element_type=F32)
        ch['cl'] = cl[:, 0:LANES] + cl[:, LANES:2 * LANES] + cl[:, 2 * LANES:3 * LANES]
    for ch in chains:
        cl, lw = ch['cl'], ch['lw']
        p_in = jnp.exp(cl)
        p_inv = jnp.exp(-cl)
        p_ex = jnp.exp(cl - lw)
        ch['pc'] = jnp.exp(jnp.sum(lw, axis=0, keepdims=True))
        p_end = ch['pc'] * p_inv
        ch['at'] = -ch['kk'] * p_ex
        ch['rt'] = ch['r'] * p_in
        ch['bt'] = ch['b'] * p_inv
        ch['kt'] = ch['kd'] * p_inv
        ch['bh'] = ch['b'] * p_end
        ch['kh'] = ch['kd'] * p_end
        ch['S'] = s_scr[ch['bi'], int(ch['rev']), ch['p']]
    for ch in chains:
        g = lax.dot_general(jnp.concatenate([ch['at'], ch['rt']], axis=0).astype(BF16),
                            jnp.concatenate([_pair_rows(ch['bt']), _pair_rows(ch['kt'])], axis=0),
                            NT, preferred_element_type=F32)
        mk = masks[ch['rev']]
        ch['lab'] = jnp.where(mk['strict'], g[0:C, 0:LANES], 0.0)
        ch['lak'] = jnp.where(mk['strict'], g[0:C, LANES:2 * LANES], 0.0)
        ch['lrb'] = jnp.where(mk['incl'], g[C:2 * C, 0:LANES], 0.0)
        ch['lrk'] = jnp.where(mk['incl'], g[C:2 * C, LANES:2 * LANES], 0.0)
        ch['vbd'] = _pair_rows(ch['v'])
    for ch in chains:
        x0 = jnp.dot(jnp.concatenate([ch['at'], ch['lak']], axis=1).astype(BF16),
                     jnp.concatenate([ch['S'].astype(BF16), ch['vbd']], axis=0), preferred_element_type=F32)
        nd = jnp.where(same_sub, ch['lab'], 0.0)
        no = jnp.where(same_sub, 0.0, ch['lab'])
        ch['zc'] = jnp.concatenate([x0, no], axis=1)
        ch['nj'] = nd
    n1 = int(math.log2(SUB))
    for lev in range(n1):
        for ch in chains:
            if lev < n1 - 1:
                rr = jnp.dot(ch['nj'].astype(BF16), _pair_rows(jnp.concatenate([ch['zc'], ch['nj']], axis=1)),
                             preferred_element_type=F32)
                ch['zc'] = ch['zc'] + rr[:, 0:2 * LANES]
                ch['nj'] = rr[:, 2 * LANES:3 * LANES]
            else:
                ch['zc'] = ch['zc'] + jnp.dot(ch['nj'].astype(BF16), _pair_rows(ch['zc']),
                                              preferred_element_type=F32)
    for ch in chains:
        ch['x'] = ch['zc'][:, 0:LANES]
        ch['mj'] = ch['zc'][:, LANES:2 * LANES]
    n2 = int(math.log2(C // SUB))
    for lev in range(n2):
        for ch in chains:
            if lev < n2 - 1:
                rr = jnp.dot(ch['mj'].astype(BF16), _pair_rows(jnp.concatenate([ch['x'], ch['mj']], axis=1)),
                             preferred_element_type=F32)
                ch['x'] = ch['x'] + rr[:, 0:LANES]
                ch['mj'] = rr[:, LANES:2 * LANES]
            else:
                ch['x'] = ch['x'] + jnp.dot(ch['mj'].astype(BF16), _pair_rows(ch['x']), preferred_element_type=F32)
    for ch in chains:
        u = ch['x']
        ubd = _pair_rows(u)
        y = jnp.dot(jnp.concatenate([ch['rt'], ch['lrb'], ch['lrk']], axis=1).astype(BF16),
                    jnp.concatenate([ch['S'].astype(BF16), ubd, ch['vbd']], axis=0), preferred_element_type=F32)
        dg = jnp.where(eye, jnp.broadcast_to(ch['pc'], (LANES, LANES)), 0.0)
        s_new = lax.dot_general(jnp.concatenate([ch['bh'], ch['kh'], dg], axis=0).astype(BF16),
                                jnp.concatenate([u, ch['v'], ch['S']], axis=0).astype(BF16),
                                TN, preferred_element_type=F32)
        s_new = jnp.where(same_head, s_new, 0.0)
        sl = slice(ch['p'] * LANES, (ch['p'] + 1) * LANES)
        if ch['rev']:
            yb_ref[ch['bi'], :, sl] = y
        else:
            yf_ref[ch['bi'], :, sl] = y
        s_scr[ch['bi'], int(ch['rev']), ch['p']] = s_new

    @pl.when(c == pl.num_programs(1) - 1)
    def _():
        sfin_ref[...] = s_scr[...]


def _rwkv_scan(r_src, r_col, k_col, v_col, kk, lw, aa, k_a, s0, *, nbb):
    nb, t_len, _ = kk.shape
    nc = t_len // CHUNK
    n_pairs = W_R // LANES

    def tok(col, rev):
        return pl.BlockSpec((nbb, CHUNK, W_R), (lambda g, c: (g, nc - 1 - c, col)) if rev else (lambda g, c: (g, c, col)))

    def dirs(z, rev):
        return pl.BlockSpec((None, nbb, CHUNK, W_R),
                            (lambda g, c: (z, g, nc - 1 - c, 0)) if rev else (lambda g, c: (z, g, c, 0)))

    st_spec = pl.BlockSpec((nbb, 2, n_pairs, LANES, LANES), lambda g, c: (g, 0, 0, 0, 0))
    in_specs, args = [], []
    for rev in (False, True):
        z = int(rev)
        in_specs += [tok(r_col, rev), tok(k_col, rev), tok(0, rev), tok(v_col, rev), dirs(z, rev), dirs(z, rev)]
        args += [r_src, r_src, kk, r_src, lw, aa]
    in_specs.append(pl.BlockSpec((1, W_R), lambda g, c: (0, 0)))
    args.append(k_a.reshape(1, W_R))
    if s0 is not None:
        in_specs.append(st_spec)
        args.append(s0)
    yshape = jax.ShapeDtypeStruct((nb, t_len, W_R), F32)
    return pl.pallas_call(
        functools.partial(_rwkv_chunk_kernel, nbb=nbb, zero_init=s0 is None),
        grid=(nb // nbb, nc),
        in_specs=in_specs,
        out_specs=[tok(0, False), tok(0, True), st_spec],
        out_shape=[yshape, yshape, jax.ShapeDtypeStruct((nb, 2, n_pairs, LANES, LANES), F32)],
        scratch_shapes=[pltpu.VMEM((nbb, 2, n_pairs, LANES, LANES), F32)],
        compiler_params=_cparams(("parallel", "arbitrary")),
        name="rwkv_scan",
    )(*args)


def _post_kernel(x_ref, yf_ref, yb_ref, bs_ref, vr_ref, gr_ref, oa_ref, ga_ref, gg_ref, mod_ref,
                 lnw_ref, lnb_ref, wa_ref, wr_ref, wo_ref, n2_ref, wi_ref, wo2_ref, o_ref, *, n_chunks, n_sub):
    d = x_ref.shape[2]
    rows = x_ref.shape[1] // n_sub
    d_ff = wo2_ref.shape[0]
    tf = d_ff // n_chunks
    subs = [dict(rs=slice(i * rows, (i + 1) * rows)) for i in range(n_sub)]

    for sb in subs:
        rs = sb['rs']
        y = yf_ref[0, rs, :] + yb_ref[0, rs, :]
        mu = _half_block_sum(y) * (1.0 / HD_R)
        yc = y - mu
        var = _half_block_sum(yc * yc) * (1.0 / HD_R)
        yn = yc * lax.rsqrt(var + EPS_GN) * lnw_ref[...] + lnb_ref[...]
        sb['o_r'] = (yn + bs_ref[0, rs, :] * vr_ref[0, rs, :]) * _sigmoid(gr_ref[0, rs, :])
    for sb in subs:
        rs = sb['rs']
        sb['merged'] = (ga_ref[0, rs, :] * _dot(oa_ref[0, rs, :], wa_ref[...])
                        + gg_ref[0, rs, :] * _dot(sb['o_r'], wr_ref[...]))
    for sb in subs:
        x1 = x_ref[0, sb['rs'], :] + mod_ref[0, :, 2 * d:3 * d] * _dot(sb['merged'], wo_ref[...])
        ms = jnp.mean(x1 * x1, axis=-1, keepdims=True)
        sb['x1'] = x1
        sb['h'] = (x1 * lax.rsqrt(ms + EPS_RMS) * n2_ref[...] * (1.0 + mod_ref[0, :, 4 * d:5 * d])
                   + mod_ref[0, :, 3 * d:4 * d]).astype(BF16)

    def up(sb, c):
        u = jnp.dot(sb['h'], wi_ref[:, c * tf:(c + 1) * tf], preferred_element_type=F32)
        g = jnp.dot(sb['h'], wi_ref[:, d_ff + c * tf:d_ff + (c + 1) * tf], preferred_element_type=F32)
        return u, g

    for sb in subs:
        sb['nxt'] = up(sb, 0)
        sb['acc'] = None
    for c in range(n_chunks):
        for sb in subs:
            u, g = sb['nxt']
            if c + 1 < n_chunks:
                sb['nxt'] = up(sb, c + 1)
            a = ((u * _sigmoid(u)) * g).astype(BF16)
            part = jnp.dot(a, wo2_ref[c * tf:(c + 1) * tf, :], preferred_element_type=F32)
            sb['acc'] = part if sb['acc'] is None else sb['acc'] + part
    for sb in subs:
        o_ref[0, sb['rs'], :] = sb['x1'] + mod_ref[0, :, 5 * d:6 * d] * sb['acc']


def _post(x, zr, gates, yf, yb, bsum, oa, mod, ln_x_w, ln_x_b, wa_bf, wr_bf, wo_bf, norm2_w, w_in_bf, w_out_bf,
          *, tm, n_chunks, n_sub):
    nb, t_len, d = x.shape
    tok = lambda w, col: pl.BlockSpec((1, tm, w), lambda b, i: (b, i, col))
    resident = lambda a: pl.BlockSpec(a.shape, lambda b, i: (0,) * a.ndim, pipeline_mode=pl.Buffered(1))
    lnw, lnb, n2 = ln_x_w.reshape(1, W_R), ln_x_b.reshape(1, W_R), norm2_w.reshape(1, d)
    return pl.pallas_call(
        functools.partial(_post_kernel, n_chunks=n_chunks, n_sub=n_sub),
        grid=(nb, t_len // tm),
        in_specs=[tok(d, 0), tok(W_R, 0), tok(W_R, 0), tok(W_R, 0), tok(COL, ZR_VR), tok(COL, ZR_GR),
                  tok(W_R, 0), tok(d, 0), tok(d, 1),
                  pl.BlockSpec((1, 1, mod.shape[2]), lambda b, i: (b, 0, 0)),
                  resident(lnw), resident(lnb), resident(wa_bf), resident(wr_bf), resident(wo_bf), resident(n2),
                  resident(w_in_bf), resident(w_out_bf)],
        out_specs=tok(d, 0),
        out_shape=jax.ShapeDtypeStruct((nb, t_len, d), F32),
        compiler_params=pltpu.CompilerParams(dimension_semantics=("parallel", "parallel"),
                                             vmem_limit_bytes=RESIDENT_VMEM_LIMIT),
        name="post",
    )(x, yf, yb, bsum, zr, zr, oa, gates, gates, mod, lnw, lnb, wa_bf, wr_bf, wo_bf, n2, w_in_bf, w_out_bf)


def _state_to_pairs(s):
    nb = s.shape[0]
    st = jnp.swapaxes(s, -1, -2).reshape(nb, 2, N_HEADS_R // 2, 2, HD_R, HD_R)
    zero = jnp.zeros_like(st[:, :, :, 0])
    top = jnp.concatenate([st[:, :, :, 0], zero], axis=-1)
    bot = jnp.concatenate([zero, st[:, :, :, 1]], axis=-1)
    return jnp.concatenate([top, bot], axis=-2)


def _pairs_to_state(sp):
    nb = sp.shape[0]
    h0 = sp[:, :, :, 0:HD_R, 0:HD_R]
    h1 = sp[:, :, :, HD_R:, HD_R:]
    st = jnp.stack([h0, h1], axis=3).reshape(nb, 2, N_HEADS_R, HD_R, HD_R)
    return jnp.swapaxes(st, -1, -2)


def _layer(x_tok, nb_seq, mod, lp, lam_init, cache, s0_pairs, rope):
    nbm, tmod, d = x_tok.shape
    t_seq = nbm * tmod // nb_seq
    plan = _tile_plan(t_seq)
    outs = _inproj(x_tok, mod, lp['norm1_w'], lp['w_in'], lp['qkw'], rope, tm=plan['tm'])
    qkv, zr, gates = outs[:3]
    kv = outs[3:5] if cache is None else None
    seq = lambda a: a.reshape(a.shape[:-3] + (nb_seq, t_seq, a.shape[-1]))
    oa = _attention(seq(qkv), cache, lp['lam'], lp['subln_w'], lam_init, tq=plan['tq'],
                    heads_per_step=plan['heads_per_step'])
    kk, lw, aa, bsum = _rwkv_prep(zr, lp['w_lora_up'], lp['a_lora_up'], lp['w0'], lp['a0'],
                                  lp['k_k'], lp['k_a'], lp['r_k'], tm=plan['tm'])
    yf, yb, s_fin = _rwkv_scan(seq(zr), ZR_RR, ZR_KR, ZR_VR, seq(kk), seq(lw), seq(aa), lp['k_a'], s0_pairs,
                               nbb=plan['nbb'])
    tokv = lambda a: a.reshape(nbm, tmod, a.shape[-1])
    y = _post(x_tok, zr, gates, tokv(yf), tokv(yb), bsum, tokv(oa), mod, lp['ln_x_w'], lp['ln_x_b'],
              lp['w_attn_br'], lp['w_rwkv_br'], lp['w_out'], lp['norm2_w'], lp['w_ffn_in'], lp['w_ffn_out'],
              tm=plan['tm'], n_chunks=plan['n_chunks'], n_sub=plan['n_sub'])
    return y, kv, s_fin


def kernel(x_prompt, x_sample, cache_k, cache_v, state_rwkv, c, c_ctx, ada_w, ada_b, norm1_w, norm2_w, w_in, q_norm_w, k_norm_w, lambda_q1, lambda_k1, lambda_q2, lambda_k2, subln_w, w_lora_up, w0, a_lora_up, a0, k_k, k_a, r_k, ln_x_w, ln_x_b, w_attn_br, w_rwkv_br, w_out, w_ffn_in, w_ffn_out):
    depth = ada_w.shape[0]
    batch, seq_len, d = x_prompt.shape
    dec_batch, dec_seq, _ = x_sample.shape
    past = cache_k.shape[2]
    qk_cols = N_HEADS_A * 2 * HD_A
    assert qk_cols == COL and W_R == COL and N_HEADS_A * VD_A == COL

    rope = _rope_tables(dec_seq)
    y_prompt = x_prompt.reshape(1, batch * seq_len, d)
    y_sample = x_sample
    ks_out, vs_out, ss_out = [], [], []
    cond_rows = 16
    for li in range(depth):
        cond = jnp.concatenate([c_ctx[None, :], c, jnp.zeros((cond_rows - 1 - dec_batch, d), F32)], axis=0)
        m = _modulation(cond, ada_w[li], ada_b[li])
        mod_ctx = m[0:1].reshape(1, 1, 6 * d)
        mod_lat = m[1:1 + dec_batch].reshape(dec_batch, 1, 6 * d)

        assert w_in.shape[2] == _tile_cols(N_TILES - 1).stop
        lp = {
            'norm1_w': norm1_w[li], 'norm2_w': norm2_w[li], 'w_in': w_in[li].astype(BF16),
            'qkw': jnp.stack([jnp.tile(q_norm_w[li], 2 * N_HEADS_A), jnp.tile(k_norm_w[li], 2 * N_HEADS_A)]),
            'lam': jnp.stack([lambda_q1[li], lambda_k1[li], lambda_q2[li], lambda_k2[li]]),
            'subln_w': subln_w[li], 'w_lora_up': w_lora_up[li], 'w0': w0[li], 'a_lora_up': a_lora_up[li],
            'a0': a0[li], 'k_k': k_k[li], 'k_a': k_a[li], 'r_k': r_k[li], 'ln_x_w': ln_x_w[li],
            'ln_x_b': ln_x_b[li], 'w_attn_br': w_attn_br[li].astype(BF16), 'w_rwkv_br': w_rwkv_br[li].astype(BF16),
            'w_out': w_out[li].astype(BF16), 'w_ffn_in': w_ffn_in[li].astype(BF16),
            'w_ffn_out': w_ffn_out[li].astype(BF16),
        }
        lam_init = 0.8 - 0.6 * math.exp(-0.3 * li)

        y_prompt, kv_ctx, s_ctx = _layer(y_prompt, batch, mod_ctx, lp, lam_init, None, None, None)
        ks_out.append(kv_ctx[0].reshape(batch, seq_len, N_HEADS_A, 2, HD_A))
        vs_out.append(kv_ctx[1].reshape(batch, seq_len, N_HEADS_A, VD_A))
        ss_out.append(_pairs_to_state(s_ctx))

        cache = (cache_k[:, li].reshape(dec_batch, past, COL).astype(BF16),
                 cache_v[:, li].reshape(dec_batch, past, COL).astype(BF16))
        s0_lat = _state_to_pairs(state_rwkv[:, li])
        y_sample, _, _ = _layer(y_sample, dec_batch, mod_lat, lp, lam_init, cache, s0_lat, rope)
    new_k = jnp.stack(ks_out, axis=1)
    new_v = jnp.stack(vs_out, axis=1)
    new_state = jnp.stack(ss_out, axis=1)
    return (y_prompt.reshape(batch, seq_len, d), y_sample, new_k, new_v, new_state)
```
